```python
import math
import jax, jax.numpy as jnp
from jax import lax
import numpy as np

D_MODEL = 4096
BATCH = 1
SEQ = 8192
DEPTH = 1
DEC_BATCH = 4
DEC_SEQ = 4096
PAST_LEN = 128

N_META = 16
GRID_W = 64
WIN_H_MAX = 8
WIN_W = 16
NA_HEADS = 16
NA_DH = 128
NA_WIDTH = NA_HEADS * NA_DH
DIFF_HEADS = 8
DIFF_DH = 128
DIFF_WIDTH = DIFF_HEADS * 2 * DIFF_DH
MIX_WIDTH = NA_WIDTH + DIFF_WIDTH
IN_WIDTH = 3 * NA_WIDTH + 3 * DIFF_WIDTH
D_FF = 11008
ROPE_THETA = 500000.0
ROPE_DIM = DIFF_DH // 4
Q_BLOCK = 128
EPS = 1e-6
SUBLN_EPS = 1e-5

kernel_name = "hymba_natten_diffattn_macaron_encoder"


def rms_norm(x, g, eps=EPS):
    xf = x.astype(jnp.float32)
    y = xf * lax.rsqrt(jnp.mean(xf * xf, axis=-1, keepdims=True) + eps)
    return (y * g.astype(jnp.float32)).astype(x.dtype)


def swiglu(x, wg, wu, wd):
    return (jax.nn.silu(x @ wg) * (x @ wu)) @ wd


def rope_partial(x, pos):
    half = ROPE_DIM // 2
    inv = ROPE_THETA ** (-2.0 * jnp.arange(half, dtype=jnp.float32) / ROPE_DIM)
    ang = pos.astype(jnp.float32)[:, None] * inv[None, :]
    cos = jnp.cos(ang)[None, :, None, :]
    sin = jnp.sin(ang)[None, :, None, :]
    xr = x[..., :ROPE_DIM].astype(jnp.float32)
    x1, x2 = xr[..., :half], xr[..., half:]
    rot = jnp.concatenate([x1 * cos - x2 * sin, x1 * sin + x2 * cos], axis=-1).astype(x.dtype)
    return jnp.concatenate([rot, x[..., ROPE_DIM:]], axis=-1)


def neighbourhood_attention(q, k, v, rpb):
    B, L, H, dh = q.shape
    S = L - N_META
    rows = S // GRID_W
    kh = min(WIN_H_MAX, rows)
    scale = NA_DH ** -0.5
    qm, km, vm = q[:, :N_META], k[:, :N_META], v[:, :N_META]
    qg = q[:, N_META:].reshape(B, rows, GRID_W, H, dh)
    kg = k[:, N_META:].reshape(B, rows, GRID_W, H, dh)
    vg = v[:, N_META:].reshape(B, rows, GRID_W, H, dh)
    row_start = np.clip(np.arange(rows) - kh // 2, 0, rows - kh)
    col_start = np.clip(np.arange(GRID_W) - WIN_W // 2, 0, GRID_W - WIN_W)
    col_idx = col_start[:, None] + np.arange(WIN_W)
    dc = col_idx - np.arange(GRID_W)[:, None] + (WIN_W - 1)

    def row_block(args):
        q_row, r, r0 = args
        k_band = lax.dynamic_slice_in_dim(kg, r0, kh, axis=1)
        v_band = lax.dynamic_slice_in_dim(vg, r0, kh, axis=1)
        k_win = k_band[:, :, col_idx]
        v_win = v_band[:, :, col_idx]
        dr = r0 + jnp.arange(kh) - r + (WIN_H_MAX - 1)
        bias = rpb[:, dr[None, :, None], dc[:, None, :]]
        s_win = jnp.einsum('bchd,bicjhd->bhcij', q_row, k_win,
                           preferred_element_type=jnp.float32) * scale + bias.astype(jnp.float32)[None]
        s_meta = jnp.einsum('bchd,bmhd->bhcm', q_row, km,
                            preferred_element_type=jnp.float32) * scale
        s = jnp.concatenate([s_win.reshape(B, H, GRID_W, kh * WIN_W), s_meta], axis=-1)
        p = jax.nn.softmax(s, axis=-1).astype(v.dtype)
        p_win = p[..., :kh * WIN_W].reshape(B, H, GRID_W, kh, WIN_W)
        p_meta = p[..., kh * WIN_W:]
        return (jnp.einsum('bhcij,bicjhd->bchd', p_win, v_win)
                + jnp.einsum('bhcm,bmhd->bchd', p_meta, vm))

    out_rows = lax.map(row_block, (jnp.moveaxis(qg, 1, 0),
                                   jnp.arange(rows, dtype=jnp.int32),
                                   jnp.asarray(row_start, dtype=jnp.int32)))
    out_real = jnp.moveaxis(out_rows, 0, 1).reshape(B, S, H, dh)
    s_mm = jnp.einsum('bqhd,bkhd->bhqk', qm, km, preferred_element_type=jnp.float32) * scale
    out_meta = jnp.einsum('bhqk,bkhd->bqhd', jax.nn.softmax(s_mm, axis=-1).astype(v.dtype), vm)
    return jnp.concatenate([out_meta, out_real], axis=1)


def diff_attend(q, k, v, lam):
    s = jnp.einsum('bqhmd,bkhmd->bhmqk', q, k, preferred_element_type=jnp.float32) * (DIFF_DH ** -0.5)
    p = jax.nn.softmax(s, axis=-1)
    a = p[:, :, 0] - lam * p[:, :, 1]
    return jnp.einsum('bhqk,bkhe->bqhe', a.astype(v.dtype), v)


def differential_attention(q, k, v, lam):
    B, L = q.shape[0], q.shape[1]
    S = L - N_META
    nb = S // Q_BLOCK
    q_real = q[:, N_META:].reshape((B, nb, Q_BLOCK) + q.shape[2:])
    out_blocks = lax.map(lambda qb: diff_attend(qb, k, v, lam), jnp.moveaxis(q_real, 1, 0))
    out_real = jnp.moveaxis(out_blocks, 0, 1).reshape(B, S, DIFF_HEADS, 2 * DIFF_DH)
    out_meta = diff_attend(q[:, :N_META], k, v, lam)
    return jnp.concatenate([out_meta, out_real], axis=1)


def token_mixing(hn, pos, layer, w_in, na_rpb, na_norm, diff_lq1, diff_lk1, diff_lq2, diff_lk2,
                 diff_norm, w_out):
    B, L, _ = hn.shape
    proj = hn @ w_in
    cuts = np.cumsum([NA_WIDTH, NA_WIDTH, NA_WIDTH, DIFF_WIDTH, DIFF_WIDTH])
    q_na, k_na, v_na, q_d, k_d, v_d = jnp.split(proj, cuts, axis=-1)
    na_shape = (B, L, NA_HEADS, NA_DH)
    na_out = neighbourhood_attention(q_na.reshape(na_shape), k_na.reshape(na_shape),
                                     v_na.reshape(na_shape), na_rpb)
    na_out = rms_norm(na_out, na_norm).reshape(B, L, NA_WIDTH)
    qd = rope_partial(q_d.reshape(B, L, DIFF_HEADS * 2, DIFF_DH), pos).reshape(B, L, DIFF_HEADS, 2, DIFF_DH)
    kd = rope_partial(k_d.reshape(B, L, DIFF_HEADS * 2, DIFF_DH), pos).reshape(B, L, DIFF_HEADS, 2, DIFF_DH)
    vd = v_d.reshape(B, L, DIFF_HEADS, 2 * DIFF_DH)
    lam_init = 0.8 - 0.6 * math.exp(-0.3 * layer)
    lam = (jnp.exp(jnp.sum(diff_lq1.astype(jnp.float32) * diff_lk1.astype(jnp.float32)))
           - jnp.exp(jnp.sum(diff_lq2.astype(jnp.float32) * diff_lk2.astype(jnp.float32)))
           + lam_init)
    d_out = differential_attention(qd, kd, vd, lam)
    d_out = (rms_norm(d_out, diff_norm, SUBLN_EPS) * (1.0 - lam_init)).reshape(B, L, DIFF_WIDTH)
    return jnp.concatenate([na_out, d_out], axis=-1) @ w_out


def encode(x, meta_tokens, ffn1_norm, ffn1_wg, ffn1_wu, ffn1_wd, mix_norm, w_in, na_rpb, na_norm,
           diff_lq1, diff_lk1, diff_lq2, diff_lk2, diff_norm, w_out, ffn2_norm, ffn2_wg, ffn2_wu,
           ffn2_wd, final_norm):
    B, S, D = x.shape
    meta = jnp.broadcast_to(meta_tokens.astype(x.dtype)[None], (B, N_META, D))
    h = jnp.concatenate([meta, x], axis=1)
    pos = jnp.arange(N_META + S, dtype=jnp.int32)
    for l in range(DEPTH):
        h = h + 0.5 * swiglu(rms_norm(h, ffn1_norm[l]), ffn1_wg[l], ffn1_wu[l], ffn1_wd[l])
        h = h + token_mixing(rms_norm(h, mix_norm[l]), pos, l, w_in[l], na_rpb[l], na_norm[l],
                             diff_lq1[l], diff_lk1[l], diff_lq2[l], diff_lk2[l], diff_norm[l], w_out[l])
        h = h + 0.5 * swiglu(rms_norm(h, ffn2_norm[l]), ffn2_wg[l], ffn2_wu[l], ffn2_wd[l])
    return rms_norm(h[:, N_META:], final_norm)


def setup_inputs(seed: int = 0) -> dict:
    key = jax.random.key(seed)
    ks = jax.random.split(key, 24)
    f32 = jnp.float32
    nrm = lambda k, shape, s: jax.random.normal(k, shape, f32) * s
    gain = lambda k, shape: 1.0 + 0.02 * jax.random.normal(k, shape, f32)
    return {
        "x_prompt": nrm(ks[0], (BATCH, SEQ, D_MODEL), 1.0),
        "x_sample": nrm(ks[1], (DEC_BATCH, DEC_SEQ, D_MODEL), 1.0),
        "meta_tokens": nrm(ks[2], (N_META, D_MODEL), 1.0),
        "ffn1_norm": gain(ks[3], (DEPTH, D_MODEL)),
        "ffn1_wg": nrm(ks[4], (DEPTH, D_MODEL, D_FF), D_MODEL ** -0.5),
        "ffn1_wu": nrm(ks[5], (DEPTH, D_MODEL, D_FF), D_MODEL ** -0.5),
        "ffn1_wd": nrm(ks[6], (DEPTH, D_FF, D_MODEL), D_FF ** -0.5),
        "mix_norm": gain(ks[7], (DEPTH, D_MODEL)),
        "w_in": nrm(ks[8], (DEPTH, D_MODEL, IN_WIDTH), D_MODEL ** -0.5),
        "na_rpb": nrm(ks[9], (DEPTH, NA_HEADS, 2 * WIN_H_MAX - 1, 2 * WIN_W - 1), 0.1),
        "na_norm": gain(ks[10], (DEPTH, NA_HEADS, NA_DH)),
        "diff_lq1": nrm(ks[11], (DEPTH, DIFF_DH), 0.1),
        "diff_lk1": nrm(ks[12], (DEPTH, DIFF_DH), 0.1),
        "diff_lq2": nrm(ks[13], (DEPTH, DIFF_DH), 0.1),
        "diff_lk2": nrm(ks[14], (DEPTH, DIFF_DH), 0.1),
        "diff_norm": gain(ks[15], (DEPTH, 2 * DIFF_DH)),
        "w_out": nrm(ks[16], (DEPTH, MIX_WIDTH, D_MODEL), MIX_WIDTH ** -0.5),
        "ffn2_norm": gain(ks[17], (DEPTH, D_MODEL)),
        "ffn2_wg": nrm(ks[18], (DEPTH, D_MODEL, D_FF), D_MODEL ** -0.5),
        "ffn2_wu": nrm(ks[19], (DEPTH, D_MODEL, D_FF), D_MODEL ** -0.5),
        "ffn2_wd": nrm(ks[20], (DEPTH, D_FF, D_MODEL), D_FF ** -0.5),
        "final_norm": gain(ks[21], (D_MODEL,)),
    }


def reference(x_prompt, x_sample, meta_tokens, ffn1_norm, ffn1_wg, ffn1_wu, ffn1_wd, mix_norm, w_in,
              na_rpb, na_norm, diff_lq1, diff_lk1, diff_lq2, diff_lk2, diff_norm, w_out, ffn2_norm,
              ffn2_wg, ffn2_wu, ffn2_wd, final_norm):
    y_prompt = encode(x_prompt, meta_tokens, ffn1_norm, ffn1_wg, ffn1_wu, ffn1_wd, mix_norm, w_in,
                      na_rpb, na_norm, diff_lq1, diff_lk1, diff_lq2, diff_lk2, diff_norm, w_out,
                      ffn2_norm, ffn2_wg, ffn2_wu, ffn2_wd, final_norm)
    y_sample = encode(x_sample, meta_tokens, ffn1_norm, ffn1_wg, ffn1_wu, ffn1_wd, mix_norm, w_in,
                      na_rpb, na_norm, diff_lq1, diff_lk1, diff_lq2, diff_lk2, diff_norm, w_out,
                      ffn2_norm, ffn2_wg, ffn2_wu, ffn2_wd, final_norm)
    return (y_prompt, y_sample)
```

```python
import functools
import math

import jax
import jax.numpy as jnp
from jax import lax
from jax.experimental import pallas as pl
from jax.experimental.pallas import tpu as pltpu

F32 = jnp.float32
BF16 = jnp.bfloat16

D_MODEL = 4096
N_META = 16
META_PAD = 128
GRID_W = 64
WIN_H = 8
WIN_W = 16
NA_HEADS = 16
HEAD_DIM = 128
NA_WIDTH = NA_HEADS * HEAD_DIM
DIFF_HEADS = 8
DIFF_WIDTH = DIFF_HEADS * 2 * HEAD_DIM
IN_WIDTH = 3 * NA_WIDTH + 3 * DIFF_WIDTH
N_CHUNKS = IN_WIDTH // HEAD_DIM
D_FF = 11008
D_FF_PAD = 11264
ROPE_THETA = 500000.0
ROPE_DIM = HEAD_DIM // 4
EPS = 1e-6
SUBLN_EPS = 1e-5
LAM_INIT = 0.8 - 0.6 * math.exp(-0.3 * 0)
ATTN_SCALE = HEAD_DIM ** -0.5
NEG = -1e30

PROMPT_LEN = 8192
SAMPLE_LEN = 4096
N_SAMPLES = 4
T_REAL = PROMPT_LEN + N_SAMPLES * SAMPLE_LEN

NA_Q = 256
NA_ROWS_Q = NA_Q // GRID_W
NA_KEYS = 3 * NA_Q
NA_GROUPS = T_REAL // NA_Q
NA_PROMPT_GROUPS = PROMPT_LEN // NA_Q
NA_SAMPLE_GROUPS = SAMPLE_LEN // NA_Q

DIFF_TQ = 512
DIFF_TK = 512
DIFF_SEG = SAMPLE_LEN
DIFF_PROMPT_QT = PROMPT_LEN // DIFF_TQ
DIFF_SAMPLE_QT = SAMPLE_LEN // DIFF_TQ
DIFF_QT = T_REAL // DIFF_TQ

MIB = 1024 * 1024
NT_DIMS = (((1,), (1,)), ((), ()))


def _params(semantics, vmem_mib):
    return pltpu.CompilerParams(dimension_semantics=semantics, vmem_limit_bytes=vmem_mib * MIB)


def _rmsnorm_kernel(x_ref, g_ref, o_ref, *, eps):
    x = x_ref[...]
    ms = jnp.mean(x * x, axis=-1, keepdims=True)
    o_ref[...] = (x * lax.rsqrt(ms + eps) * g_ref[...]).astype(o_ref.dtype)


def _rmsnorm(x, gain, out_dtype, tm):
    m, d = x.shape
    return pl.pallas_call(
        functools.partial(_rmsnorm_kernel, eps=EPS),
        grid=(m // tm,),
        in_specs=[pl.BlockSpec((tm, d), lambda i: (i, 0)),
                  pl.BlockSpec((1, d), lambda i: (0, 0))],
        out_specs=pl.BlockSpec((tm, d), lambda i: (i, 0)),
        out_shape=jax.ShapeDtypeStruct((m, d), out_dtype),
        compiler_params=_params(("parallel",), 32),
        name="rmsnorm",
    )(x, gain.reshape(1, d))


def _gate_up_kernel(x_ref, wg_ref, wu_ref, o_ref):
    x = x_ref[...]
    g = jnp.dot(x, wg_ref[...], preferred_element_type=F32)
    u = jnp.dot(x, wu_ref[...], preferred_element_type=F32)
    o_ref[...] = (0.5 * jax.nn.silu(g) * u).astype(o_ref.dtype)


def _gate_up(xn, wg, wu, tm, tn):
    m, d = xn.shape
    f = wg.shape[1]
    return pl.pallas_call(
        _gate_up_kernel,
        grid=(m // tm, f // tn),
        in_specs=[pl.BlockSpec((tm, d), lambda i, j: (i, 0)),
                  pl.BlockSpec((d, tn), lambda i, j: (0, j)),
                  pl.BlockSpec((d, tn), lambda i, j: (0, j))],
        out_specs=pl.BlockSpec((tm, tn), lambda i, j: (i, j)),
        out_shape=jax.ShapeDtypeStruct((m, f), BF16),
        compiler_params=_params(("parallel", "arbitrary"), 52),
        name="ffn_gate_up",
    )(xn, wg, wu)


def _down_res_kernel(a_ref, w_ref, r_ref, o_ref):
    @pl.when(pl.program_id(2) == 0)
    def _():
        o_ref[...] = r_ref[...]

    o_ref[...] += jnp.dot(a_ref[...], w_ref[...], preferred_element_type=F32)


def _down_res(a, w, res, tm, tn, tk):
    m, f = a.shape
    d = w.shape[1]
    return pl.pallas_call(
        _down_res_kernel,
        grid=(m // tm, d // tn, f // tk),
        in_specs=[pl.BlockSpec((tm, tk), lambda i, j, k: (i, k)),
                  pl.BlockSpec((tk, tn), lambda i, j, k: (k, j)),
                  pl.BlockSpec((tm, tn), lambda i, j, k: (i, j))],
        out_specs=pl.BlockSpec((tm, tn), lambda i, j, k: (i, j)),
        out_shape=jax.ShapeDtypeStruct((m, d), F32),
        compiler_params=_params(("parallel", "parallel", "arbitrary"), 44),
        name="ffn_down_res",
    )(a, w, res)


def _ffn_half_step(h, gain, wg, wu, wd, tm):
    xn = _rmsnorm(h, gain, BF16, min(tm, 256))
    a = _gate_up(xn, wg, wu, tm, 512)
    return _down_res(a, wd, h, tm, 1024, 1024)


def _in_proj_kernel(x_ref, w_ref, cs_ref, cos_ref, sa_ref, sb_ref, o_ref, *, rope_lo, rope_hi):
    j = pl.program_id(1)
    y = jnp.dot(x_ref[...], w_ref[...], preferred_element_type=F32) * cs_ref[...]
    n_chunks = o_ref.shape[0]
    is_rope = jnp.logical_and(j >= rope_lo, j < rope_hi)

    @pl.when(is_rope)
    def _():
        for c in range(n_chunks):
            yc = y[:, c * HEAD_DIM:(c + 1) * HEAD_DIM]
            upper = pltpu.roll(yc, HEAD_DIM - ROPE_DIM // 2, 1)
            lower = pltpu.roll(yc, ROPE_DIM // 2, 1)
            o_ref[c] = (yc * cos_ref[...] + upper * sa_ref[...] + lower * sb_ref[...]).astype(o_ref.dtype)

    @pl.when(jnp.logical_not(is_rope))
    def _():
        for c in range(n_chunks):
            o_ref[c] = y[:, c * HEAD_DIM:(c + 1) * HEAD_DIM].astype(o_ref.dtype)


def _in_proj(xn, w, col_scale, rope_tabs, tm, tn):
    m, d = xn.shape
    n = w.shape[1]
    cpb = tn // HEAD_DIM
    rope_lo = (3 * NA_WIDTH) // tn
    rope_hi = (3 * NA_WIDTH + 2 * DIFF_WIDTH) // tn
    tab_spec = pl.BlockSpec((tm, HEAD_DIM), lambda i, j: (i, 0))
    return pl.pallas_call(
        functools.partial(_in_proj_kernel, rope_lo=rope_lo, rope_hi=rope_hi),
        grid=(m // tm, n // tn),
        in_specs=[pl.BlockSpec((tm, d), lambda i, j: (i, 0)),
                  pl.BlockSpec((d, tn), lambda i, j: (0, j)),
                  pl.BlockSpec((1, tn), lambda i, j: (0, j)),
                  tab_spec, tab_spec, tab_spec],
        out_specs=pl.BlockSpec((cpb, tm, HEAD_DIM), lambda i, j: (j, i, 0)),
        out_shape=jax.ShapeDtypeStruct((n // HEAD_DIM, m, HEAD_DIM), BF16),
        compiler_params=_params(("parallel", "arbitrary"), 48),
        name="in_proj_rope",
    )(xn, w, col_scale, *rope_tabs)


def _rope_tables(pos):
    half = ROPE_DIM // 2
    inv = ROPE_THETA ** (-2.0 * jnp.arange(half, dtype=F32) / ROPE_DIM)
    ang = pos.astype(F32)[:, None] * inv[None, :]
    cos, sin = jnp.cos(ang), jnp.sin(ang)
    n = pos.shape[0]
    ones = jnp.ones((n, HEAD_DIM - ROPE_DIM), F32)
    zeros_h = jnp.zeros((n, half), F32)
    zeros_r = jnp.zeros((n, HEAD_DIM - ROPE_DIM), F32)
    cos_t = jnp.concatenate([cos, cos, ones], axis=1)
    sa_t = jnp.concatenate([-sin, zeros_h, zeros_r], axis=1)
    sb_t = jnp.concatenate([zeros_h, sin, zeros_r], axis=1)
    return cos_t, sa_t, sb_t


def _na_bias_kernel(rpb_ref, o_ref):
    h = pl.program_id(0)
    shape = (GRID_W, 2 * GRID_W)
    c = lax.broadcasted_iota(jnp.int32, shape, 0)
    lane = lax.broadcasted_iota(jnp.int32, shape, 1)
    second = lane >= GRID_W
    cc = jnp.bitwise_and(lane, GRID_W - 1)
    rel = cc - c + (WIN_W - 1)
    c0 = jnp.clip(c - WIN_W // 2, 0, GRID_W - WIN_W)
    col_ok = jnp.logical_and(cc >= c0, cc < c0 + WIN_W)
    n_dr = 2 * WIN_H - 1
    n_dc = 2 * WIN_W - 1
    base = h * (n_dr * n_dc)

    pair = []
    for d in range(n_dr - 1):
        val = jnp.zeros(shape, F32)
        for k in range(n_dc):
            w0 = rpb_ref[base + d * n_dc + k]
            w1 = rpb_ref[base + (d + 1) * n_dc + k]
            val = jnp.where(rel == k, jnp.where(second, w1, w0), val)
        pair.append(jnp.where(col_ok, val, NEG))

    key_rows = 3 * NA_ROWS_Q
    row_of_lane = second.astype(jnp.int32)
    for qr in range(NA_ROWS_Q):
        for jp in range(key_rows // 2):
            j = 2 * jp + row_of_lane
            tile = pair[2 * jp + (NA_ROWS_Q - 1) - qr]
            rows = pl.ds(qr * GRID_W, GRID_W)
            cols = pl.ds(jp * 2 * GRID_W, 2 * GRID_W)
            first_ok = j >= NA_ROWS_Q
            mid_ok = jnp.logical_and(j >= qr, j <= qr + WIN_H - 1)
            last_ok = j <= WIN_H - 1
            o_ref[0, 0, rows, cols] = jnp.where(first_ok, tile, NEG)
            o_ref[1, 0, rows, cols] = jnp.where(mid_ok, tile, NEG)
            o_ref[2, 0, rows, cols] = jnp.where(last_ok, tile, NEG)


def _na_bias(rpb):
    return pl.pallas_call(
        _na_bias_kernel,
        grid=(NA_HEADS,),
        in_specs=[pl.BlockSpec(memory_space=pltpu.SMEM)],
        out_specs=pl.BlockSpec((3, 1, NA_Q, NA_KEYS), lambda h: (0, h, 0, 0)),
        out_shape=jax.ShapeDtypeStruct((3, NA_HEADS, NA_Q, NA_KEYS), F32),
        compiler_params=_params(("arbitrary",), 32),
        name="na_bias_table",
    )(rpb.reshape(-1))


def _na_kernel(q_ref, k0_ref, k1_ref, k2_ref, v0_ref, v1_ref, v2_ref, km_ref, vm_ref, b_ref, g_ref, o_ref):
    q = q_ref[0]
    lane = lax.broadcasted_iota(jnp.int32, (1, META_PAD), 1)
    meta_mask = jnp.where(lane < N_META, 0.0, NEG).astype(F32)

    s = [lax.dot_general(q, k_ref[0], NT_DIMS, preferred_element_type=F32)
         + b_ref[0, 0, :, j * NA_Q:(j + 1) * NA_Q]
         for j, k_ref in enumerate((k0_ref, k1_ref, k2_ref))]
    s.append(lax.dot_general(q, km_ref[0], NT_DIMS, preferred_element_type=F32) + meta_mask)

    mx = functools.reduce(jnp.maximum, [jnp.max(x, axis=1, keepdims=True) for x in s])
    p = [jnp.exp(x - mx) for x in s]
    denom = functools.reduce(jnp.add, [jnp.sum(x, axis=1, keepdims=True) for x in p])
    vals = (v0_ref[0], v1_ref[0], v2_ref[0], vm_ref[0])
    o = functools.reduce(jnp.add, [jnp.dot(x.astype(BF16), v, preferred_element_type=F32)
                                   for x, v in zip(p, vals)])
    o = o / denom
    ms = jnp.mean(o * o, axis=-1, keepdims=True)
    o_ref[...] = (o * lax.rsqrt(ms + EPS) * g_ref[0]).astype(o_ref.dtype)


def _na_edge_type(g):
    local = jnp.where(g < NA_PROMPT_GROUPS, g, jnp.bitwise_and(g - NA_PROMPT_GROUPS, NA_SAMPLE_GROUPS - 1))
    last = jnp.where(g < NA_PROMPT_GROUPS, NA_PROMPT_GROUPS - 1, NA_SAMPLE_GROUPS - 1)
    return jnp.where(local == 0, 0, jnp.where(local == last, 2, 1))


def _na_attention(p3, pm3, bias, gain):
    def tok_spec(chunk0, shift):
        def index(h, g):
            return (chunk0 + h, jnp.clip(g + shift, 0, NA_GROUPS - 1), 0)
        return pl.BlockSpec((1, NA_Q, HEAD_DIM), index)

    def meta_spec(chunk0):
        return pl.BlockSpec((1, META_PAD, HEAD_DIM), lambda h, g: (chunk0 + h, 0, 0))

    k0, v0 = NA_HEADS, 2 * NA_HEADS
    return pl.pallas_call(
        _na_kernel,
        grid=(NA_HEADS, NA_GROUPS),
        in_specs=[tok_spec(0, 0),
                  tok_spec(k0, -1), tok_spec(k0, 0), tok_spec(k0, 1),
                  tok_spec(v0, -1), tok_spec(v0, 0), tok_spec(v0, 1),
                  meta_spec(k0), meta_spec(v0),
                  pl.BlockSpec((1, 1, NA_Q, NA_KEYS), lambda h, g: (_na_edge_type(g), h, 0, 0)),
                  pl.BlockSpec((1, 1, HEAD_DIM), lambda h, g: (h, 0, 0))],
        out_specs=pl.BlockSpec((NA_Q, HEAD_DIM), lambda h, g: (g, h)),
        out_shape=jax.ShapeDtypeStruct((T_REAL, NA_WIDTH), BF16),
        compiler_params=_params(("parallel", "arbitrary"), 32),
        name="na_attention",
    )(p3, p3, p3, p3, p3, p3, p3, pm3, pm3, bias, gain.reshape(NA_HEADS, 1, HEAD_DIM))


def _diff_kernel(lq1_ref, lk1_ref, lq2_ref, lk2_ref, g_ref, q_ref, ka_ref, kb_ref, va_ref, vb_ref,
                 km_ref, vm_ref, o_ref, m_s, l_s, acc_s):
    qt = pl.program_id(1)
    lane = lax.broadcasted_iota(jnp.int32, (1, META_PAD), 1)
    meta_mask = jnp.where(lane < N_META, 0.0, NEG).astype(F32)

    vm = jnp.concatenate([vm_ref[0], vm_ref[1]], axis=1)
    for mp in range(2):
        s = lax.dot_general(q_ref[mp], km_ref[mp], NT_DIMS, preferred_element_type=F32) + meta_mask
        mx = jnp.max(s, axis=1, keepdims=True)
        p = jnp.exp(s - mx)
        m_s[mp] = mx
        l_s[mp] = jnp.sum(p, axis=1, keepdims=True)
        acc_s[mp] = jnp.dot(p.astype(BF16), vm, preferred_element_type=F32)

    def segment(k_ref, v_ref):
        def body(c, carry):
            rows = pl.ds(pl.multiple_of(c * DIFF_TK, DIFF_TK), DIFF_TK)
            v = jnp.concatenate([v_ref[0, rows, :], v_ref[1, rows, :]], axis=1)
            for mp in range(2):
                s = lax.dot_general(q_ref[mp], k_ref[mp, rows, :], NT_DIMS, preferred_element_type=F32)
                m_old = m_s[mp]
                m_new = jnp.maximum(m_old, jnp.max(s, axis=1, keepdims=True))
                alpha = jnp.exp(m_old - m_new)
                p = jnp.exp(s - m_new)
                l_s[mp] = alpha * l_s[mp] + jnp.sum(p, axis=1, keepdims=True)
                acc_s[mp] = alpha * acc_s[mp] + jnp.dot(p.astype(BF16), v, preferred_element_type=F32)
                m_s[mp] = m_new
            return carry

        lax.fori_loop(0, DIFF_SEG // DIFF_TK, body, 0)

    segment(ka_ref, va_ref)

    @pl.when(qt < DIFF_PROMPT_QT)
    def _():
        segment(kb_ref, vb_ref)

    lam = (jnp.exp(jnp.sum(lq1_ref[...] * lk1_ref[...], axis=1, keepdims=True))
           - jnp.exp(jnp.sum(lq2_ref[...] * lk2_ref[...], axis=1, keepdims=True)) + LAM_INIT)
    o = acc_s[0] / l_s[0] - lam * (acc_s[1] / l_s[1])
    ms = jnp.mean(o * o, axis=-1, keepdims=True)
    o_ref[...] = ((o * lax.rsqrt(ms + SUBLN_EPS) * g_ref[...]) * (1.0 - LAM_INIT)).astype(o_ref.dtype)


def _diff_attention(p3, pm3, lq1, lk1, lq2, lk2, gain):
    q0 = (3 * NA_WIDTH) // (2 * HEAD_DIM)
    k0 = q0 + DIFF_HEADS
    v0 = k0 + DIFF_HEADS
    seg_per_q = DIFF_SEG // DIFF_TQ

    def seg_a(qt):
        return jnp.where(qt < DIFF_PROMPT_QT, 0, qt // seg_per_q)

    def seg_b(qt):
        return jnp.where(qt < DIFF_PROMPT_QT, 1, qt // seg_per_q)

    def kv_spec(chunk0, seg):
        return pl.BlockSpec((2, DIFF_SEG, HEAD_DIM), lambda h, qt: (chunk0 + h, seg(qt), 0))

    def meta_spec(chunk0):
        return pl.BlockSpec((2, META_PAD, HEAD_DIM), lambda h, qt: (chunk0 + h, 0, 0))

    vec_spec = pl.BlockSpec((1, HEAD_DIM), lambda h, qt: (0, 0))
    return pl.pallas_call(
        _diff_kernel,
        grid=(DIFF_HEADS, DIFF_QT),
        in_specs=[vec_spec, vec_spec, vec_spec, vec_spec,
                  pl.BlockSpec((1, 2 * HEAD_DIM), lambda h, qt: (0, 0)),
                  pl.BlockSpec((2, DIFF_TQ, HEAD_DIM), lambda h, qt: (q0 + h, qt, 0)),
                  kv_spec(k0, seg_a), kv_spec(k0, seg_b), kv_spec(v0, seg_a), kv_spec(v0, seg_b),
                  meta_spec(k0), meta_spec(v0)],
        out_specs=pl.BlockSpec((DIFF_TQ, 2 * HEAD_DIM), lambda h, qt: (qt, h)),
        out_shape=jax.ShapeDtypeStruct((T_REAL, DIFF_WIDTH), BF16),
        scratch_shapes=[pltpu.VMEM((2, DIFF_TQ, 1), F32),
                        pltpu.VMEM((2, DIFF_TQ, 1), F32),
                        pltpu.VMEM((2, DIFF_TQ, 2 * HEAD_DIM), F32)],
        compiler_params=_params(("parallel", "arbitrary"), 48),
        name="diff_attention",
    )(lq1.reshape(1, -1), lk1.reshape(1, -1), lq2.reshape(1, -1), lk2.reshape(1, -1),
      gain.reshape(1, -1), p3, p3, p3, p3, p3, pm3, pm3)


def _out_proj_kernel(na_ref, d_ref, w_ref, r_ref, o_ref):
    y = jnp.dot(na_ref[...], w_ref[:NA_WIDTH, :], preferred_element_type=F32)
    y += jnp.dot(d_ref[...], w_ref[NA_WIDTH:, :], preferred_element_type=F32)
    o_ref[...] = r_ref[...] + y


def _out_proj(na, d, w, res, tm, tn):
    m = na.shape[0]
    n = w.shape[1]
    return pl.pallas_call(
        _out_proj_kernel,
        grid=(m // tm, n // tn),
        in_specs=[pl.BlockSpec((tm, NA_WIDTH), lambda i, j: (i, 0)),
                  pl.BlockSpec((tm, DIFF_WIDTH), lambda i, j: (i, 0)),
                  pl.BlockSpec((NA_WIDTH + DIFF_WIDTH, tn), lambda i, j: (0, j)),
                  pl.BlockSpec((tm, tn), lambda i, j: (i, j))],
        out_specs=pl.BlockSpec((tm, tn), lambda i, j: (i, j)),
        out_shape=jax.ShapeDtypeStruct((m, n), F32),
        compiler_params=_params(("parallel", "arbitrary"), 48),
        name="out_proj_res",
    )(na, d, w, res)


def _pad_cols_bf16(w):
    return jnp.pad(w, ((0, 0), (0, D_FF_PAD - D_FF))).astype(BF16)


def _pad_rows_bf16(w):
    return jnp.pad(w, ((0, D_FF_PAD - D_FF), (0, 0))).astype(BF16)


def kernel(x_prompt, x_sample, meta_tokens, ffn1_norm, ffn1_wg, ffn1_wu, ffn1_wd, mix_norm, w_in, na_rpb, na_norm, diff_lq1, diff_lk1, diff_lq2, diff_lk2, diff_norm, w_out, ffn2_norm, ffn2_wg, ffn2_wu, ffn2_wd, final_norm):
    x = jnp.concatenate([x_prompt.reshape(PROMPT_LEN, D_MODEL),
                         x_sample.reshape(N_SAMPLES * SAMPLE_LEN, D_MODEL)], axis=0)
    meta = jnp.pad(meta_tokens, ((0, META_PAD - N_META), (0, 0)))

    wg1, wu1, wd1 = _pad_cols_bf16(ffn1_wg[0]), _pad_cols_bf16(ffn1_wu[0]), _pad_rows_bf16(ffn1_wd[0])
    wg2, wu2, wd2 = _pad_cols_bf16(ffn2_wg[0]), _pad_cols_bf16(ffn2_wu[0]), _pad_rows_bf16(ffn2_wd[0])
    w_in_b = w_in[0].astype(BF16)
    w_out_b = w_out[0].astype(BF16)

    col = jnp.arange(IN_WIDTH)
    is_q = jnp.logical_or(col < NA_WIDTH,
                          jnp.logical_and(col >= 3 * NA_WIDTH, col < 3 * NA_WIDTH + DIFF_WIDTH))
    col_scale = jnp.where(is_q, ATTN_SCALE, 1.0).astype(F32).reshape(1, IN_WIDTH)

    pos_real = jnp.concatenate([jnp.arange(PROMPT_LEN, dtype=jnp.int32)]
                               + [jnp.arange(SAMPLE_LEN, dtype=jnp.int32)] * N_SAMPLES) + N_META
    pos_meta = jnp.arange(META_PAD, dtype=jnp.int32)

    h1 = _ffn_half_step(x, ffn1_norm[0], wg1, wu1, wd1, 1024)
    h1m = _ffn_half_step(meta, ffn1_norm[0], wg1, wu1, wd1, META_PAD)

    xn = _rmsnorm(h1, mix_norm[0], BF16, 256)
    xnm = _rmsnorm(h1m, mix_norm[0], BF16, META_PAD)
    p3 = _in_proj(xn, w_in_b, col_scale, _rope_tables(pos_real), 1024, 512)
    pm3 = _in_proj(xnm, w_in_b, col_scale, _rope_tables(pos_meta), META_PAD, 512)
    bias = _na_bias(na_rpb[0])
    na = _na_attention(p3, pm3, bias, na_norm[0])
    dd = _diff_attention(p3, pm3, diff_lq1[0], diff_lk1[0], diff_lq2[0], diff_lk2[0], diff_norm[0])
    h2 = _out_proj(na, dd, w_out_b, h1, 1024, 512)

    h3 = _ffn_half_step(h2, ffn2_norm[0], wg2, wu2, wd2, 1024)
    y = _rmsnorm(h3, final_norm, F32, 256)

    y_prompt = y[:PROMPT_LEN].reshape(1, PROMPT_LEN, D_MODEL)
    y_sample = y[PROMPT_LEN:].reshape(N_SAMPLES, SAMPLE_LEN, D_MODEL)
    return (y_prompt, y_sample)
```

```python
import functools
import math

import jax
import jax.numpy as jnp
from jax import lax
from jax.experimental import pallas as pl
from jax.experimental.pallas import tpu as pltpu

F32 = jnp.float32
BF16 = jnp.bfloat16

D_MODEL = 4096
N_META = 16
META_PAD = 128
GRID_W = 64
WIN_H = 8
WIN_W = 16
NA_HEADS = 16
HEAD_DIM = 128
NA_WIDTH = NA_HEADS * HEAD_DIM
DIFF_HEADS = 8
DIFF_WIDTH = DIFF_HEADS * 2 * HEAD_DIM
IN_WIDTH = 3 * NA_WIDTH + 3 * DIFF_WIDTH
N_CHUNKS = IN_WIDTH // HEAD_DIM
D_FF = 11008
D_FF_PAD = 11264
ROPE_THETA = 500000.0
ROPE_DIM = HEAD_DIM // 4
EPS = 1e-6
SUBLN_EPS = 1e-5
LAM_INIT = 0.8 - 0.6 * math.exp(-0.3 * 0)
ATTN_SCALE = HEAD_DIM ** -0.5
LOG2E = math.log2(math.e)
NEG = -1e30

PROMPT_LEN = 8192
SAMPLE_LEN = 4096
N_SAMPLES = 4
T_REAL = PROMPT_LEN + N_SAMPLES * SAMPLE_LEN

NA_Q = 256
NA_ROWS_Q = NA_Q // GRID_W
NA_KEYS = 3 * NA_Q
NA_GROUPS = T_REAL // NA_Q
NA_HEADS_PER_STEP = 4
NA_PROMPT_GROUPS = PROMPT_LEN // NA_Q
NA_SAMPLE_GROUPS = SAMPLE_LEN // NA_Q

DIFF_TK = 1024

MIB = 1024 * 1024
NT_DIMS = (((1,), (1,)), ((), ()))
PROJ_SUB = 256


def _params(semantics, vmem_mib):
    return pltpu.CompilerParams(dimension_semantics=semantics, vmem_limit_bytes=vmem_mib * MIB)


def _rmsnorm_kernel(x_ref, g_ref, o_ref, *, eps):
    x = x_ref[...]
    ms = jnp.mean(x * x, axis=-1, keepdims=True)
    o_ref[...] = (x * lax.rsqrt(ms + eps) * g_ref[...]).astype(o_ref.dtype)


def _rmsnorm(x, gain, out_dtype, tm):
    m, d = x.shape
    return pl.pallas_call(
        functools.partial(_rmsnorm_kernel, eps=EPS),
        grid=(m // tm,),
        in_specs=[pl.BlockSpec((tm, d), lambda i: (i, 0)),
                  pl.BlockSpec((1, d), lambda i: (0, 0))],
        out_specs=pl.BlockSpec((tm, d), lambda i: (i, 0)),
        out_shape=jax.ShapeDtypeStruct((m, d), out_dtype),
        compiler_params=_params(("parallel",), 32),
        name="rmsnorm",
    )(x, gain.reshape(1, d))


def _rmsnorm_concat_kernel(xp_ref, xs_ref, g_ref, o_ref, x_ref, *, eps, prompt_tiles):
    def emit(src_ref):
        x = src_ref[...]
        ms = jnp.mean(x * x, axis=-1, keepdims=True)
        o_ref[...] = (x * lax.rsqrt(ms + eps) * g_ref[...]).astype(o_ref.dtype)
        x_ref[...] = x

    i = pl.program_id(0)
    pl.when(i < prompt_tiles)(lambda: emit(xp_ref))
    pl.when(i >= prompt_tiles)(lambda: emit(xs_ref))


def _rmsnorm_concat(xp, xs, gain, tm):
    d = xp.shape[1]
    pt = xp.shape[0] // tm
    m = xp.shape[0] + xs.shape[0]
    row_spec = pl.BlockSpec((tm, d), lambda i: (i, 0))
    return pl.pallas_call(
        functools.partial(_rmsnorm_concat_kernel, eps=EPS, prompt_tiles=pt),
        grid=(m // tm,),
        in_specs=[pl.BlockSpec((tm, d), lambda i: (jnp.minimum(i, pt - 1), 0)),
                  pl.BlockSpec((tm, d), lambda i: (jnp.maximum(i - pt, 0), 0)),
                  pl.BlockSpec((1, d), lambda i: (0, 0))],
        out_specs=[row_spec, row_spec],
        out_shape=[jax.ShapeDtypeStruct((m, d), BF16), jax.ShapeDtypeStruct((m, d), F32)],
        compiler_params=_params(("arbitrary",), 40),
        name="rmsnorm_concat",
    )(xp, xs, gain.reshape(1, d))


def _rmsnorm_split_kernel(x_ref, g_ref, op_ref, os_ref, *, eps, prompt_tiles):
    x = x_ref[...]
    ms = jnp.mean(x * x, axis=-1, keepdims=True)
    y = x * lax.rsqrt(ms + eps) * g_ref[...]
    i = pl.program_id(0)

    @pl.when(i < prompt_tiles)
    def _():
        op_ref[...] = y

    @pl.when(i >= prompt_tiles)
    def _():
        os_ref[...] = y


def _rmsnorm_split(x, gain, n_prompt, tm):
    m, d = x.shape
    pt = n_prompt // tm
    return pl.pallas_call(
        functools.partial(_rmsnorm_split_kernel, eps=EPS, prompt_tiles=pt),
        grid=(m // tm,),
        in_specs=[pl.BlockSpec((tm, d), lambda i: (i, 0)),
                  pl.BlockSpec((1, d), lambda i: (0, 0))],
        out_specs=[pl.BlockSpec((tm, d), lambda i: (jnp.minimum(i, pt - 1), 0)),
                   pl.BlockSpec((tm, d), lambda i: (jnp.maximum(i - pt, 0), 0))],
        out_shape=[jax.ShapeDtypeStruct((n_prompt, d), F32), jax.ShapeDtypeStruct((m - n_prompt, d), F32)],
        compiler_params=_params(("arbitrary",), 40),
        name="rmsnorm_split",
    )(x, gain.reshape(1, d))


def _gate_up_kernel(x_ref, wg_ref, wu_ref, o_ref):
    x = x_ref[...]
    g = jnp.dot(x, wg_ref[...], preferred_element_type=F32)
    u = jnp.dot(x, wu_ref[...], preferred_element_type=F32)
    o_ref[...] = (0.5 * jax.nn.silu(g) * u).astype(o_ref.dtype)


def _gate_up(xn, wg, wu, tm, tn):
    m, d = xn.shape
    f = wg.shape[1]
    return pl.pallas_call(
        _gate_up_kernel,
        grid=(m // tm, f // tn),
        in_specs=[pl.BlockSpec((tm, d), lambda i, j: (i, 0)),
                  pl.BlockSpec((d, tn), lambda i, j: (0, j)),
                  pl.BlockSpec((d, tn), lambda i, j: (0, j))],
        out_specs=pl.BlockSpec((tm, tn), lambda i, j: (i, j)),
        out_shape=jax.ShapeDtypeStruct((m, f), BF16),
        compiler_params=_params(("parallel", "arbitrary"), 52),
        name="ffn_gate_up",
    )(xn, wg, wu)


def _down_res_kernel(a_ref, w_ref, r_ref, o_ref):
    o_ref[...] = r_ref[...] + jnp.dot(a_ref[...], w_ref[...], preferred_element_type=F32)


def _down_res(a, w, res, tm, tn):
    m, f = a.shape
    d = w.shape[1]
    return pl.pallas_call(
        _down_res_kernel,
        grid=(m // tm, d // tn),
        in_specs=[pl.BlockSpec((tm, f), lambda i, j: (i, 0)),
                  pl.BlockSpec((f, tn), lambda i, j: (0, j)),
                  pl.BlockSpec((tm, tn), lambda i, j: (i, j))],
        out_specs=pl.BlockSpec((tm, tn), lambda i, j: (i, j)),
        out_shape=jax.ShapeDtypeStruct((m, d), F32),
        compiler_params=_params(("parallel", "arbitrary"), 56),
        name="ffn_down_res",
    )(a, w, res)


def _swiglu_res(xn, res, wg, wu, wd, tm_up, tm_down):
    a = _gate_up(xn, wg, wu, tm_up, 512)
    return _down_res(a, wd, res, tm_down, 256)


def _in_proj_kernel(x_ref, w_ref, cs_ref, cos_ref, sa_ref, sb_ref, o_ref, *, rope_lo, rope_hi):
    j = pl.program_id(1)
    tn = w_ref.shape[1]
    is_rope = jnp.logical_and(j >= rope_lo, j < rope_hi)

    def body(rope):
        x = x_ref[...]
        for sub in range(tn // PROJ_SUB):
            cols = slice(sub * PROJ_SUB, (sub + 1) * PROJ_SUB)
            y = jnp.dot(x, w_ref[:, cols], preferred_element_type=F32) * cs_ref[:, cols]
            for c in range(PROJ_SUB // HEAD_DIM):
                yc = y[:, c * HEAD_DIM:(c + 1) * HEAD_DIM]
                if rope:
                    upper = pltpu.roll(yc, HEAD_DIM - ROPE_DIM // 2, 1)
                    lower = pltpu.roll(yc, ROPE_DIM // 2, 1)
                    yc = yc * cos_ref[...] + upper * sa_ref[...] + lower * sb_ref[...]
                o_ref[sub * (PROJ_SUB // HEAD_DIM) + c] = yc.astype(o_ref.dtype)

    pl.when(is_rope)(lambda: body(True))
    pl.when(jnp.logical_not(is_rope))(lambda: body(False))


def _in_proj(xn, w, col_scale, rope_tabs, tm, tn):
    m, d = xn.shape
    n = w.shape[1]
    cpb = tn // HEAD_DIM
    rope_lo = (3 * NA_WIDTH) // tn
    rope_hi = (3 * NA_WIDTH + 2 * DIFF_WIDTH) // tn
    tab_spec = pl.BlockSpec((tm, HEAD_DIM), lambda i, j: (i, 0))
    return pl.pallas_call(
        functools.partial(_in_proj_kernel, rope_lo=rope_lo, rope_hi=rope_hi),
        grid=(m // tm, n // tn),
        in_specs=[pl.BlockSpec((tm, d), lambda i, j: (i, 0)),
                  pl.BlockSpec((d, tn), lambda i, j: (0, j)),
                  pl.BlockSpec((1, tn), lambda i, j: (0, j)),
                  tab_spec, tab_spec, tab_spec],
        out_specs=pl.BlockSpec((cpb, tm, HEAD_DIM), lambda i, j: (j, i, 0)),
        out_shape=jax.ShapeDtypeStruct((n // HEAD_DIM, m, HEAD_DIM), BF16),
        compiler_params=_params(("parallel", "arbitrary"), 56),
        name="in_proj_rope",
    )(xn, w, col_scale, *rope_tabs)


def _rope_tables(pos):
    half = ROPE_DIM // 2
    inv = ROPE_THETA ** (-2.0 * jnp.arange(half, dtype=F32) / ROPE_DIM)
    ang = pos.astype(F32)[:, None] * inv[None, :]
    cos, sin = jnp.cos(ang), jnp.sin(ang)
    n = pos.shape[0]
    ones = jnp.ones((n, HEAD_DIM - ROPE_DIM), F32)
    zeros_h = jnp.zeros((n, half), F32)
    zeros_r = jnp.zeros((n, HEAD_DIM - ROPE_DIM), F32)
    cos_t = jnp.concatenate([cos, cos, ones], axis=1)
    sa_t = jnp.concatenate([-sin, zeros_h, zeros_r], axis=1)
    sb_t = jnp.concatenate([zeros_h, sin, zeros_r], axis=1)
    return cos_t, sa_t, sb_t


def _na_bias_kernel(rpb_ref, o_ref):
    h = pl.program_id(0)
    shape = (GRID_W, 2 * GRID_W)
    c = lax.broadcasted_iota(jnp.int32, shape, 0)
    lane = lax.broadcasted_iota(jnp.int32, shape, 1)
    second = lane >= GRID_W
    cc = jnp.bitwise_and(lane, GRID_W - 1)
    rel = cc - c + (WIN_W - 1)
    c0 = jnp.clip(c - WIN_W // 2, 0, GRID_W - WIN_W)
    col_ok = jnp.logical_and(cc >= c0, cc < c0 + WIN_W)
    n_dr = 2 * WIN_H - 1
    n_dc = 2 * WIN_W - 1
    base = h * (n_dr * n_dc)

    pair = []
    for d in range(n_dr - 1):
        val = jnp.zeros(shape, F32)
        for k in range(n_dc):
            w0 = rpb_ref[base + d * n_dc + k]
            w1 = rpb_ref[base + (d + 1) * n_dc + k]
            val = jnp.where(rel == k, jnp.where(second, w1, w0), val)
        pair.append(jnp.where(col_ok, val, NEG))

    key_rows = 3 * NA_ROWS_Q
    row_of_lane = second.astype(jnp.int32)
    for qr in range(NA_ROWS_Q):
        for jp in range(key_rows // 2):
            j = 2 * jp + row_of_lane
            tile = pair[2 * jp + (NA_ROWS_Q - 1) - qr]
            rows = pl.ds(qr * GRID_W, GRID_W)
            cols = pl.ds(jp * 2 * GRID_W, 2 * GRID_W)
            first_ok = j >= NA_ROWS_Q
            mid_ok = jnp.logical_and(j >= qr, j <= qr + WIN_H - 1)
            last_ok = j <= WIN_H - 1
            o_ref[0, 0, rows, cols] = jnp.where(first_ok, tile, NEG)
            o_ref[1, 0, rows, cols] = jnp.where(mid_ok, tile, NEG)
            o_ref[2, 0, rows, cols] = jnp.where(last_ok, tile, NEG)


def _na_bias(rpb):
    return pl.pallas_call(
        _na_bias_kernel,
        grid=(NA_HEADS,),
        in_specs=[pl.BlockSpec(memory_space=pltpu.SMEM)],
        out_specs=pl.BlockSpec((3, 1, NA_Q, NA_KEYS), lambda h: (0, h, 0, 0)),
        out_shape=jax.ShapeDtypeStruct((3, NA_HEADS, NA_Q, NA_KEYS), F32),
        compiler_params=_params(("arbitrary",), 32),
        name="na_bias_table",
    )(rpb.reshape(-1))


def _na_kernel(q_ref, k0_ref, k1_ref, k2_ref, v0_ref, v1_ref, v2_ref, km_ref, vm_ref, b_ref, g_ref, o_ref):
    lane = lax.broadcasted_iota(jnp.int32, (1, META_PAD), 1)
    meta_mask = jnp.where(lane < N_META, 0.0, NEG).astype(F32)

    for hh in range(q_ref.shape[0]):
        q = q_ref[hh]
        s = [lax.dot_general(q, k_ref[hh], NT_DIMS, preferred_element_type=F32)
             + b_ref[0, hh, :, j * NA_Q:(j + 1) * NA_Q]
             for j, k_ref in enumerate((k0_ref, k1_ref, k2_ref))]
        s.append(lax.dot_general(q, km_ref[hh], NT_DIMS, preferred_element_type=F32) + meta_mask)

        mx = functools.reduce(jnp.maximum, [jnp.max(x, axis=1, keepdims=True) for x in s])
        p = [jnp.exp(x - mx) for x in s]
        denom = functools.reduce(jnp.add, [jnp.sum(x, axis=1, keepdims=True) for x in p])
        vals = (v0_ref[hh], v1_ref[hh], v2_ref[hh], vm_ref[hh])
        o = functools.reduce(jnp.add, [jnp.dot(x.astype(BF16), v, preferred_element_type=F32)
                                       for x, v in zip(p, vals)])
        o = o / denom
        ms = jnp.mean(o * o, axis=-1, keepdims=True)
        o_ref[:, hh * HEAD_DIM:(hh + 1) * HEAD_DIM] = (o * lax.rsqrt(ms + EPS) * g_ref[hh]).astype(o_ref.dtype)


def _na_edge_type(g):
    local = jnp.where(g < NA_PROMPT_GROUPS, g, jnp.bitwise_and(g - NA_PROMPT_GROUPS, NA_SAMPLE_GROUPS - 1))
    last = jnp.where(g < NA_PROMPT_GROUPS, NA_PROMPT_GROUPS - 1, NA_SAMPLE_GROUPS - 1)
    return jnp.where(local == 0, 0, jnp.where(local == last, 2, 1))


def _na_attention(p3, pm3, bias, gain):
    hps = NA_HEADS_PER_STEP
    steps_per_group = NA_HEADS // hps

    def tok_spec(block0, shift):
        def index(hb, g):
            return (block0 + hb, jnp.clip(g + shift, 0, NA_GROUPS - 1), 0)
        return pl.BlockSpec((hps, NA_Q, HEAD_DIM), index)

    def meta_spec(block0):
        return pl.BlockSpec((hps, META_PAD, HEAD_DIM), lambda hb, g: (block0 + hb, 0, 0))

    k0, v0 = steps_per_group, 2 * steps_per_group
    return pl.pallas_call(
        _na_kernel,
        grid=(steps_per_group, NA_GROUPS),
        in_specs=[tok_spec(0, 0),
                  tok_spec(k0, -1), tok_spec(k0, 0), tok_spec(k0, 1),
                  tok_spec(v0, -1), tok_spec(v0, 0), tok_spec(v0, 1),
                  meta_spec(k0), meta_spec(v0),
                  pl.BlockSpec((1, hps, NA_Q, NA_KEYS), lambda hb, g: (_na_edge_type(g), hb, 0, 0)),
                  pl.BlockSpec((hps, 1, HEAD_DIM), lambda hb, g: (hb, 0, 0))],
        out_specs=pl.BlockSpec((NA_Q, hps * HEAD_DIM), lambda hb, g: (g, hb)),
        out_shape=jax.ShapeDtypeStruct((T_REAL, NA_WIDTH), BF16),
        compiler_params=_params(("parallel", "arbitrary"), 40),
        name="na_attention",
    )(p3, p3, p3, p3, p3, p3, p3, pm3, pm3, bias, gain.reshape(NA_HEADS, 1, HEAD_DIM))


def _diff_kernel(lq1_ref, lk1_ref, lq2_ref, lk2_ref, g_ref, q_ref, k_ref, v_ref,
                 km_ref, vm_ref, o_ref, s_s, sm_s, m_s, l_s, acc_s):
    tq = q_ref.shape[1]
    n_chunks = k_ref.shape[1] // DIFF_TK
    lane_tiles = DIFF_TK // HEAD_DIM
    lane = lax.broadcasted_iota(jnp.int32, (1, META_PAD), 1)
    meta_mask = jnp.where(lane < N_META, 0.0, NEG).astype(F32)

    for mp in range(2):
        s = lax.dot_general(q_ref[mp], km_ref[mp], NT_DIMS, preferred_element_type=F32) + meta_mask
        sm_s[mp] = s
        m_s[mp] = s

    for c in range(n_chunks):
        rows = slice(c * DIFF_TK, (c + 1) * DIFF_TK)
        for mp in range(2):
            s = lax.dot_general(q_ref[mp], k_ref[mp, rows, :], NT_DIMS, preferred_element_type=F32)
            s_s[mp, c] = s
            tiles = [s[:, t * HEAD_DIM:(t + 1) * HEAD_DIM] for t in range(lane_tiles)]
            m_s[mp] = jnp.maximum(m_s[mp], functools.reduce(jnp.maximum, tiles))

    for mp in range(2):
        m_s[mp] = jnp.broadcast_to(jnp.max(m_s[mp], axis=1, keepdims=True), (tq, HEAD_DIM))

    pm = [jnp.exp2(sm_s[mp] - m_s[mp]) for mp in range(2)]
    vm = jnp.concatenate([vm_ref[0], vm_ref[1]], axis=1)
    for mp in range(2):
        l_s[mp] = pm[mp]
    acc_s[...] = jnp.dot(jnp.concatenate(pm, axis=0).astype(BF16), vm, preferred_element_type=F32)

    for c in range(n_chunks):
        rows = slice(c * DIFF_TK, (c + 1) * DIFF_TK)
        v = jnp.concatenate([v_ref[0, rows, :], v_ref[1, rows, :]], axis=1)
        ps = []
        for mp in range(2):
            m_b = m_s[mp]
            lsum = l_s[mp]
            parts = []
            for t in range(lane_tiles):
                p = jnp.exp2(s_s[mp, c, :, t * HEAD_DIM:(t + 1) * HEAD_DIM] - m_b)
                lsum = lsum + p
                parts.append(p.astype(BF16))
            l_s[mp] = lsum
            ps.append(jnp.concatenate(parts, axis=1))
        acc_s[...] += jnp.dot(jnp.concatenate(ps, axis=0), v, preferred_element_type=F32)

    lam =(jnp.exp(jnp.sum(lq1_ref[...] * lk1_ref[...], axis=1, keepdims=True))
           - jnp.exp(jnp.sum(lq2_ref[...] * lk2_ref[...], axis=1, keepdims=True)) + LAM_INIT)
    l1 = jnp.sum(l_s[0], axis=1, keepdims=True)
    l2 = jnp.sum(l_s[1], axis=1, keepdims=True)
    o = acc_s[:tq, :] / l1 - lam * (acc_s[tq:, :] / l2)
    ms = jnp.mean(o * o, axis=-1, keepdims=True)
    o_ref[...] = ((o * lax.rsqrt(ms + SUBLN_EPS) * g_ref[...]) * (1.0 - LAM_INIT)).astype(o_ref.dtype)


def _diff_attention(p3, pm3, lq1, lk1, lq2, lk2, gain, row0, seq_len, n_seq, tq):
    q0 = (3 * NA_WIDTH) // (2 * HEAD_DIM)
    k0 = q0 + DIFF_HEADS
    v0 = k0 + DIFF_HEADS
    qt_per_seq = seq_len // tq
    n_chunks = seq_len // DIFF_TK

    def kv_spec(chunk0):
        return pl.BlockSpec((2, seq_len, HEAD_DIM),
                            lambda h, qt: (chunk0 + h, row0 // seq_len + qt // qt_per_seq, 0))

    def meta_spec(chunk0):
        return pl.BlockSpec((2, META_PAD, HEAD_DIM), lambda h, qt: (chunk0 + h, 0, 0))

    vec_spec = pl.BlockSpec((1, HEAD_DIM), lambda h, qt: (0, 0))
    return pl.pallas_call(
        _diff_kernel,
        grid=(DIFF_HEADS, n_seq * qt_per_seq),
        in_specs=[vec_spec, vec_spec, vec_spec, vec_spec,
                  pl.BlockSpec((1, 2 * HEAD_DIM), lambda h, qt: (0, 0)),
                  pl.BlockSpec((2, tq, HEAD_DIM), lambda h, qt: (q0 + h, row0 // tq + qt, 0)),
                  kv_spec(k0), kv_spec(v0), meta_spec(k0), meta_spec(v0)],
        out_specs=pl.BlockSpec((tq, 2 * HEAD_DIM), lambda h, qt: (qt, h)),
        out_shape=jax.ShapeDtypeStruct((n_seq * seq_len, DIFF_WIDTH), BF16),
        scratch_shapes=[pltpu.VMEM((2, n_chunks, tq, DIFF_TK), F32),
                        pltpu.VMEM((2, tq, META_PAD), F32),
                        pltpu.VMEM((2, tq, HEAD_DIM), F32),
                        pltpu.VMEM((2, tq, HEAD_DIM), F32),
                        pltpu.VMEM((2 * tq, 2 * HEAD_DIM), F32)],
        compiler_params=_params(("parallel", "arbitrary"), 52),
        name="diff_attention",
    )(lq1.reshape(1, -1), lk1.reshape(1, -1), lq2.reshape(1, -1), lk2.reshape(1, -1),
      gain.reshape(1, -1), p3, p3, p3, pm3, pm3)


def _out_proj_kernel(na_ref, dp_ref, ds_ref, w_ref, r_ref, o_ref, *, prompt_tiles):
    def body(d_ref):
        y = jnp.dot(na_ref[...], w_ref[:NA_WIDTH, :], preferred_element_type=F32)
        y += jnp.dot(d_ref[...], w_ref[NA_WIDTH:, :], preferred_element_type=F32)
        o_ref[...] = r_ref[...] + y

    i = pl.program_id(0)
    pl.when(i < prompt_tiles)(lambda: body(dp_ref))
    pl.when(i >= prompt_tiles)(lambda: body(ds_ref))


def _out_proj(na, d_prompt, d_sample, w, res, tm, tn):
    m = na.shape[0]
    n = w.shape[1]
    pt = d_prompt.shape[0] // tm
    return pl.pallas_call(
        functools.partial(_out_proj_kernel, prompt_tiles=pt),
        grid=(m // tm, n // tn),
        in_specs=[pl.BlockSpec((tm, NA_WIDTH), lambda i, j: (i, 0)),
                  pl.BlockSpec((tm, DIFF_WIDTH), lambda i, j: (jnp.minimum(i, pt - 1), 0)),
                  pl.BlockSpec((tm, DIFF_WIDTH), lambda i, j: (jnp.maximum(i - pt, 0), 0)),
                  pl.BlockSpec((NA_WIDTH + DIFF_WIDTH, tn), lambda i, j: (0, j)),
                  pl.BlockSpec((tm, tn), lambda i, j: (i, j))],
        out_specs=pl.BlockSpec((tm, tn), lambda i, j: (i, j)),
        out_shape=jax.ShapeDtypeStruct((m, n), F32),
        compiler_params=_params(("parallel", "arbitrary"), 52),
        name="out_proj_res",
    )(na, d_prompt, d_sample, w, res)


def _pad_cols_bf16(w):
    return jnp.pad(w, ((0, 0), (0, D_FF_PAD - D_FF))).astype(BF16)


def _pad_rows_bf16(w):
    return jnp.pad(w, ((0, D_FF_PAD - D_FF), (0, 0))).astype(BF16)


def kernel(x_prompt, x_sample, meta_tokens, ffn1_norm, ffn1_wg, ffn1_wu, ffn1_wd, mix_norm, w_in, na_rpb, na_norm, diff_lq1, diff_lk1, diff_lq2, diff_lk2, diff_norm, w_out, ffn2_norm, ffn2_wg, ffn2_wu, ffn2_wd, final_norm):
    xp = x_prompt.reshape(PROMPT_LEN, D_MODEL)
    xs = x_sample.reshape(N_SAMPLES * SAMPLE_LEN, D_MODEL)
    meta = jnp.pad(meta_tokens, ((0, META_PAD - N_META), (0, 0)))

    wg1, wu1, wd1 = _pad_cols_bf16(ffn1_wg[0]), _pad_cols_bf16(ffn1_wu[0]), _pad_rows_bf16(ffn1_wd[0])
    wg2, wu2, wd2 = _pad_cols_bf16(ffn2_wg[0]), _pad_cols_bf16(ffn2_wu[0]), _pad_rows_bf16(ffn2_wd[0])
    w_in_b = w_in[0].astype(BF16)
    w_out_b = w_out[0].astype(BF16)

    col = jnp.arange(IN_WIDTH)
    is_q_na = col < NA_WIDTH
    is_q_diff = jnp.logical_and(col >= 3 * NA_WIDTH, col < 3 * NA_WIDTH + DIFF_WIDTH)
    col_scale = jnp.where(is_q_na, ATTN_SCALE, jnp.where(is_q_diff, ATTN_SCALE * LOG2E, 1.0))
    col_scale = col_scale.astype(F32).reshape(1, IN_WIDTH)

    pos_real = jnp.concatenate([jnp.arange(PROMPT_LEN, dtype=jnp.int32)]
                               + [jnp.arange(SAMPLE_LEN, dtype=jnp.int32)] * N_SAMPLES) + N_META
    pos_meta = jnp.arange(META_PAD, dtype=jnp.int32)

    xn1, x = _rmsnorm_concat(xp, xs, ffn1_norm[0], 256)
    h1 = _swiglu_res(xn1, x, wg1, wu1, wd1, 1024, 768)
    xnm1 = _rmsnorm(meta, ffn1_norm[0], BF16, META_PAD)
    h1m = _swiglu_res(xnm1, meta, wg1, wu1, wd1, META_PAD, META_PAD)

    xn = _rmsnorm(h1, mix_norm[0], BF16, 256)
    xnm = _rmsnorm(h1m, mix_norm[0], BF16, META_PAD)
    p3 = _in_proj(xn, w_in_b, col_scale, _rope_tables(pos_real), 1024, 1024)
    pm3 = _in_proj(xnm, w_in_b, col_scale, _rope_tables(pos_meta), META_PAD, 1024)
    bias = _na_bias(na_rpb[0])
    na = _na_attention(p3, pm3, bias, na_norm[0])
    lam_args = (diff_lq1[0], diff_lk1[0], diff_lq2[0], diff_lk2[0], diff_norm[0])
    dd_prompt = _diff_attention(p3, pm3, *lam_args, row0=0, seq_len=PROMPT_LEN, n_seq=1, tq=256)
    dd_sample = _diff_attention(p3, pm3, *lam_args, row0=PROMPT_LEN, seq_len=SAMPLE_LEN, n_seq=N_SAMPLES, tq=512)
    h2 = _out_proj(na, dd_prompt, dd_sample, w_out_b, h1, 1024, 512)

    xn2 = _rmsnorm(h2, ffn2_norm[0], BF16, 256)
    h3 = _swiglu_res(xn2, h2, wg2, wu2, wd2, 1024, 768)
    y_prompt, y_sample = _rmsnorm_split(h3, final_norm, PROMPT_LEN, 256)
    return (y_prompt.reshape(1, PROMPT_LEN, D_MODEL), y_sample.reshape(N_SAMPLES, SAMPLE_LEN, D_MODEL))
```

```python
import functools
import math

import jax
import jax.numpy as jnp
from jax import lax
from jax.experimental import pallas as pl
from jax.experimental.pallas import tpu as pltpu

F32 = jnp.float32
BF16 = jnp.bfloat16

D_MODEL = 4096
N_META = 16
META_PAD = 128
GRID_W = 64
WIN_H = 8
WIN_W = 16
NA_HEADS = 16
HEAD_DIM = 128
NA_WIDTH = NA_HEADS * HEAD_DIM
DIFF_HEADS = 8
DIFF_WIDTH = DIFF_HEADS * 2 * HEAD_DIM
IN_WIDTH = 3 * NA_WIDTH + 3 * DIFF_WIDTH
N_CHUNKS = IN_WIDTH // HEAD_DIM
D_FF = 11008
ROPE_THETA = 500000.0
ROPE_DIM = HEAD_DIM // 4
EPS = 1e-6
SUBLN_EPS = 1e-5
LAM_INIT = 0.8 - 0.6 * math.exp(-0.3 * 0)
ATTN_SCALE = HEAD_DIM ** -0.5
LOG2E = math.log2(math.e)
NEG = -1e30

PROMPT_LEN = 8192
SAMPLE_LEN = 4096
N_SAMPLES = 4
T_REAL = PROMPT_LEN + N_SAMPLES * SAMPLE_LEN

NA_Q = 256
NA_ROWS_Q = NA_Q // GRID_W
NA_KEYS = 3 * NA_Q
NA_GROUPS = T_REAL // NA_Q
NA_HEADS_PER_STEP = 8
NA_PROMPT_GROUPS = PROMPT_LEN // NA_Q
NA_SAMPLE_GROUPS = SAMPLE_LEN // NA_Q

DIFF_TK = 1024
DIFF_STREAM_ROWS = 128

MIB = 1024 * 1024
NT_DIMS = (((1,), (1,)), ((), ()))
PROJ_SUB = 256


def _params(semantics, vmem_mib):
    return pltpu.CompilerParams(dimension_semantics=semantics, vmem_limit_bytes=vmem_mib * MIB)


def _rmsnorm_kernel(x_ref, g_ref, o_ref, *, eps):
    x = x_ref[...]
    ms = jnp.mean(x * x, axis=-1, keepdims=True)
    o_ref[...] = (x * lax.rsqrt(ms + eps) * g_ref[...]).astype(o_ref.dtype)


def _rmsnorm(x, gain, out_dtype, tm):
    m, d = x.shape
    return pl.pallas_call(
        functools.partial(_rmsnorm_kernel, eps=EPS),
        grid=(m // tm,),
        in_specs=[pl.BlockSpec((tm, d), lambda i: (i, 0)),
                  pl.BlockSpec((1, d), lambda i: (0, 0))],
        out_specs=pl.BlockSpec((tm, d), lambda i: (i, 0)),
        out_shape=jax.ShapeDtypeStruct((m, d), out_dtype),
        compiler_params=_params(("parallel",), 32),
        name="rmsnorm",
    )(x, gain.reshape(1, d))


def _rmsnorm_concat_kernel(xp_ref, xs_ref, g_ref, o_ref, x_ref, *, eps, prompt_tiles):
    def emit(src_ref):
        x = src_ref[...]
        ms = jnp.mean(x * x, axis=-1, keepdims=True)
        o_ref[...] = (x * lax.rsqrt(ms + eps) * g_ref[...]).astype(o_ref.dtype)
        x_ref[...] = x

    i = pl.program_id(0)
    pl.when(i < prompt_tiles)(lambda: emit(xp_ref))
    pl.when(i >= prompt_tiles)(lambda: emit(xs_ref))


def _rmsnorm_concat(xp, xs, gain, tm):
    d = xp.shape[1]
    pt = xp.shape[0] // tm
    m = xp.shape[0] + xs.shape[0]
    row_spec = pl.BlockSpec((tm, d), lambda i: (i, 0))
    return pl.pallas_call(
        functools.partial(_rmsnorm_concat_kernel, eps=EPS, prompt_tiles=pt),
        grid=(m // tm,),
        in_specs=[pl.BlockSpec((tm, d), lambda i: (jnp.minimum(i, pt - 1), 0)),
                  pl.BlockSpec((tm, d), lambda i: (jnp.maximum(i - pt, 0), 0)),
                  pl.BlockSpec((1, d), lambda i: (0, 0))],
        out_specs=[row_spec, row_spec],
        out_shape=[jax.ShapeDtypeStruct((m, d), BF16), jax.ShapeDtypeStruct((m, d), F32)],
        compiler_params=_params(("arbitrary",), 40),
        name="rmsnorm_concat",
    )(xp, xs, gain.reshape(1, d))


def _rmsnorm_split_kernel(x_ref, g_ref, op_ref, os_ref, *, eps, prompt_tiles):
    x = x_ref[...]
    ms = jnp.mean(x * x, axis=-1, keepdims=True)
    y = x * lax.rsqrt(ms + eps) * g_ref[...]
    i = pl.program_id(0)

    @pl.when(i < prompt_tiles)
    def _():
        op_ref[...] = y

    @pl.when(i >= prompt_tiles)
    def _():
        os_ref[...] = y


def _rmsnorm_split(x, gain, n_prompt, tm):
    m, d = x.shape
    pt = n_prompt // tm
    return pl.pallas_call(
        functools.partial(_rmsnorm_split_kernel, eps=EPS, prompt_tiles=pt),
        grid=(m // tm,),
        in_specs=[pl.BlockSpec((tm, d), lambda i: (i, 0)),
                  pl.BlockSpec((1, d), lambda i: (0, 0))],
        out_specs=[pl.BlockSpec((tm, d), lambda i: (jnp.minimum(i, pt - 1), 0)),
                   pl.BlockSpec((tm, d), lambda i: (jnp.maximum(i - pt, 0), 0))],
        out_shape=[jax.ShapeDtypeStruct((n_prompt, d), F32), jax.ShapeDtypeStruct((m - n_prompt, d), F32)],
        compiler_params=_params(("arbitrary",), 40),
        name="rmsnorm_split",
    )(x, gain.reshape(1, d))


def _gate_up_kernel(x_ref, wg_ref, wu_ref, o_ref):
    x = x_ref[...]
    g = jnp.dot(x, wg_ref[...], preferred_element_type=F32)
    u = jnp.dot(x, wu_ref[...], preferred_element_type=F32)
    o_ref[...] = (0.5 * jax.nn.silu(g) * u).astype(o_ref.dtype)


def _gate_up(xn, wg, wu, tm, tn):
    m, d = xn.shape
    f = wg.shape[1]
    return pl.pallas_call(
        _gate_up_kernel,
        grid=(m // tm, pl.cdiv(f, tn)),
        in_specs=[pl.BlockSpec((tm, d), lambda i, j: (i, 0)),
                  pl.BlockSpec((d, tn), lambda i, j: (0, j)),
                  pl.BlockSpec((d, tn), lambda i, j: (0, j))],
        out_specs=pl.BlockSpec((tm, tn), lambda i, j: (i, j)),
        out_shape=jax.ShapeDtypeStruct((m, f), BF16),
        compiler_params=_params(("parallel", "arbitrary"), 52),
        name="ffn_gate_up",
    )(xn, wg, wu)


def _down_res_kernel(a_ref, w_ref, r_ref, o_ref):
    o_ref[...] = r_ref[...] + jnp.dot(a_ref[...], w_ref[...], preferred_element_type=F32)


def _down_res(a, w, res, tm, tn):
    m, f = a.shape
    d = w.shape[1]
    return pl.pallas_call(
        _down_res_kernel,
        grid=(m // tm, d // tn),
        in_specs=[pl.BlockSpec((tm, f), lambda i, j: (i, 0)),
                  pl.BlockSpec((f, tn), lambda i, j: (0, j)),
                  pl.BlockSpec((tm, tn), lambda i, j: (i, j))],
        out_specs=pl.BlockSpec((tm, tn), lambda i, j: (i, j)),
        out_shape=jax.ShapeDtypeStruct((m, d), F32),
        compiler_params=_params(("parallel", "arbitrary"), 56),
        name="ffn_down_res",
    )(a, w, res)


def _swiglu_res(xn, res, wg, wu, wd, tm_up, tm_down):
    a = _gate_up(xn, wg, wu, tm_up, 512)
    return _down_res(a, wd, res, tm_down, 256)


def _in_proj_kernel(x_ref, w_ref, cs_ref, cos_ref, sa_ref, sb_ref, o_ref, *, rope_lo, rope_hi):
    j = pl.program_id(1)
    tn = w_ref.shape[1]
    is_rope = jnp.logical_and(j >= rope_lo, j < rope_hi)

    def body(rope):
        x = x_ref[...]
        for sub in range(tn // PROJ_SUB):
            cols = slice(sub * PROJ_SUB, (sub + 1) * PROJ_SUB)
            y = jnp.dot(x, w_ref[:, cols], preferred_element_type=F32) * cs_ref[:, cols]
            for c in range(PROJ_SUB // HEAD_DIM):
                yc = y[:, c * HEAD_DIM:(c + 1) * HEAD_DIM]
                if rope:
                    upper = pltpu.roll(yc, HEAD_DIM - ROPE_DIM // 2, 1)
                    lower = pltpu.roll(yc, ROPE_DIM // 2, 1)
                    yc = yc * cos_ref[...] + upper * sa_ref[...] + lower * sb_ref[...]
                o_ref[sub * (PROJ_SUB // HEAD_DIM) + c] = yc.astype(o_ref.dtype)

    pl.when(is_rope)(lambda: body(True))
    pl.when(jnp.logical_not(is_rope))(lambda: body(False))


def _in_proj(xn, w, col_scale, rope_tabs, tm, tn):
    m, d = xn.shape
    n = w.shape[1]
    cpb = tn // HEAD_DIM
    rope_lo = (3 * NA_WIDTH) // tn
    rope_hi = (3 * NA_WIDTH + 2 * DIFF_WIDTH) // tn
    tab_spec = pl.BlockSpec((tm, HEAD_DIM), lambda i, j: (i, 0))
    return pl.pallas_call(
        functools.partial(_in_proj_kernel, rope_lo=rope_lo, rope_hi=rope_hi),
        grid=(m // tm, n // tn),
        in_specs=[pl.BlockSpec((tm, d), lambda i, j: (i, 0)),
                  pl.BlockSpec((d, tn), lambda i, j: (0, j)),
                  pl.BlockSpec((1, tn), lambda i, j: (0, j)),
                  tab_spec, tab_spec, tab_spec],
        out_specs=pl.BlockSpec((cpb, tm, HEAD_DIM), lambda i, j: (j, i, 0)),
        out_shape=jax.ShapeDtypeStruct((n // HEAD_DIM, m, HEAD_DIM), BF16),
        compiler_params=_params(("parallel", "arbitrary"), 56),
        name="in_proj_rope",
    )(xn, w, col_scale, *rope_tabs)


def _rope_tables(pos):
    half = ROPE_DIM // 2
    inv = ROPE_THETA ** (-2.0 * jnp.arange(half, dtype=F32) / ROPE_DIM)
    ang = pos.astype(F32)[:, None] * inv[None, :]
    cos, sin = jnp.cos(ang), jnp.sin(ang)
    n = pos.shape[0]
    ones = jnp.ones((n, HEAD_DIM - ROPE_DIM), F32)
    zeros_h = jnp.zeros((n, half), F32)
    zeros_r = jnp.zeros((n, HEAD_DIM - ROPE_DIM), F32)
    cos_t = jnp.concatenate([cos, cos, ones], axis=1)
    sa_t = jnp.concatenate([-sin, zeros_h, zeros_r], axis=1)
    sb_t = jnp.concatenate([zeros_h, sin, zeros_r], axis=1)
    return cos_t, sa_t, sb_t


def _na_bias_kernel(rpb_ref, o_ref):
    h = pl.program_id(0)
    shape = (GRID_W, 2 * GRID_W)
    c = lax.broadcasted_iota(jnp.int32, shape, 0)
    lane = lax.broadcasted_iota(jnp.int32, shape, 1)
    second = lane >= GRID_W
    cc = jnp.bitwise_and(lane, GRID_W - 1)
    rel = cc - c + (WIN_W - 1)
    c0 = jnp.clip(c - WIN_W // 2, 0, GRID_W - WIN_W)
    col_ok = jnp.logical_and(cc >= c0, cc < c0 + WIN_W)
    n_dr = 2 * WIN_H - 1
    n_dc = 2 * WIN_W - 1
    base = h * (n_dr * n_dc)

    pair = []
    for d in range(n_dr - 1):
        val = jnp.zeros(shape, F32)
        for k in range(n_dc):
            w0 = rpb_ref[base + d * n_dc + k]
            w1 = rpb_ref[base + (d + 1) * n_dc + k]
            val = jnp.where(rel == k, jnp.where(second, w1, w0), val)
        pair.append(jnp.where(col_ok, val, NEG))

    key_rows = 3 * NA_ROWS_Q
    row_of_lane = second.astype(jnp.int32)
    for qr in range(NA_ROWS_Q):
        for jp in range(key_rows // 2):
            j = 2 * jp + row_of_lane
            tile = pair[2 * jp + (NA_ROWS_Q - 1) - qr]
            rows = pl.ds(qr * GRID_W, GRID_W)
            cols = pl.ds(jp * 2 * GRID_W, 2 * GRID_W)
            first_ok = j >= NA_ROWS_Q
            mid_ok = jnp.logical_and(j >= qr, j <= qr + WIN_H - 1)
            last_ok = j <= WIN_H - 1
            o_ref[0, 0, rows, cols] = jnp.where(first_ok, tile, NEG)
            o_ref[1, 0, rows, cols] = jnp.where(mid_ok, tile, NEG)
            o_ref[2, 0, rows, cols] = jnp.where(last_ok, tile, NEG)


def _na_bias(rpb):
    return pl.pallas_call(
        _na_bias_kernel,
        grid=(NA_HEADS,),
        in_specs=[pl.BlockSpec(memory_space=pltpu.SMEM)],
        out_specs=pl.BlockSpec((3, 1, NA_Q, NA_KEYS), lambda h: (0, h, 0, 0)),
        out_shape=jax.ShapeDtypeStruct((3, NA_HEADS, NA_Q, NA_KEYS), F32),
        compiler_params=_params(("arbitrary",), 32),
        name="na_bias_table",
    )(rpb.reshape(-1))


def _na_kernel(q_ref, k0_ref, k1_ref, k2_ref, v0_ref, v1_ref, v2_ref, km_ref, vm_ref, b_ref, g_ref, o_ref):
    lane = lax.broadcasted_iota(jnp.int32, (1, META_PAD), 1)
    meta_mask = jnp.where(lane < N_META, 0.0, NEG).astype(F32)

    for hh in range(q_ref.shape[0]):
        q = q_ref[hh]
        s = [lax.dot_general(q, k_ref[hh], NT_DIMS, preferred_element_type=F32)
             + b_ref[0, hh, :, j * NA_Q:(j + 1) * NA_Q]
             for j, k_ref in enumerate((k0_ref, k1_ref, k2_ref))]
        s.append(lax.dot_general(q, km_ref[hh], NT_DIMS, preferred_element_type=F32) + meta_mask)

        mx = functools.reduce(jnp.maximum, [jnp.max(x, axis=1, keepdims=True) for x in s])
        p = [jnp.exp(x - mx) for x in s]
        denom = functools.reduce(jnp.add, [jnp.sum(x, axis=1, keepdims=True) for x in p])
        vals = (v0_ref[hh], v1_ref[hh], v2_ref[hh], vm_ref[hh])
        o = functools.reduce(jnp.add, [jnp.dot(x.astype(BF16), v, preferred_element_type=F32)
                                       for x, v in zip(p, vals)])
        o = o * (1.0 / denom)
        ms = jnp.mean(o * o, axis=-1, keepdims=True)
        o_ref[:, hh * HEAD_DIM:(hh + 1) * HEAD_DIM] = (o * lax.rsqrt(ms + EPS) * g_ref[hh]).astype(o_ref.dtype)


def _na_edge_type(g):
    local = jnp.where(g < NA_PROMPT_GROUPS, g, jnp.bitwise_and(g - NA_PROMPT_GROUPS, NA_SAMPLE_GROUPS - 1))
    last = jnp.where(g < NA_PROMPT_GROUPS, NA_PROMPT_GROUPS - 1, NA_SAMPLE_GROUPS - 1)
    return jnp.where(local == 0, 0, jnp.where(local == last, 2, 1))


def _na_attention(p3, pm3, bias, gain):
    hps = NA_HEADS_PER_STEP
    steps_per_group = NA_HEADS // hps

    def tok_spec(block0, shift):
        def index(hb, g):
            return (block0 + hb, jnp.clip(g + shift, 0, NA_GROUPS - 1), 0)
        return pl.BlockSpec((hps, NA_Q, HEAD_DIM), index)

    def meta_spec(block0):
        return pl.BlockSpec((hps, META_PAD, HEAD_DIM), lambda hb, g: (block0 + hb, 0, 0))

    k0, v0 = steps_per_group, 2 * steps_per_group
    return pl.pallas_call(
        _na_kernel,
        grid=(steps_per_group, NA_GROUPS),
        in_specs=[tok_spec(0, 0),
                  tok_spec(k0, -1), tok_spec(k0, 0), tok_spec(k0, 1),
                  tok_spec(v0, -1), tok_spec(v0, 0), tok_spec(v0, 1),
                  meta_spec(k0), meta_spec(v0),
                  pl.BlockSpec((1, hps, NA_Q, NA_KEYS), lambda hb, g: (_na_edge_type(g), hb, 0, 0)),
                  pl.BlockSpec((hps, 1, HEAD_DIM), lambda hb, g: (hb, 0, 0))],
        out_specs=pl.BlockSpec((NA_Q, hps * HEAD_DIM), lambda hb, g: (g, hb)),
        out_shape=jax.ShapeDtypeStruct((T_REAL, NA_WIDTH), BF16),
        compiler_params=_params(("parallel", "arbitrary"), 40),
        name="na_attention",
    )(p3, p3, p3, p3, p3, p3, p3, pm3, pm3, bias, gain.reshape(NA_HEADS, 1, HEAD_DIM))


def _diff_kernel(lq1_ref, lk1_ref, lq2_ref, lk2_ref, g_ref, q_ref, k_ref, v_ref,
                 km_ref, vm_ref, o_ref, s_s, sm_s, m_s, l_s, acc_s):
    tq = q_ref.shape[1]
    n_chunks = k_ref.shape[1] // DIFF_TK
    lane_tiles = DIFF_TK // HEAD_DIM
    lane = lax.broadcasted_iota(jnp.int32, (1, META_PAD), 1)
    meta_mask = jnp.where(lane < N_META, 0.0, NEG).astype(F32)

    for mp in range(2):
        for r0 in range(0, tq, DIFF_STREAM_ROWS):
            qrows = slice(r0, r0 + DIFF_STREAM_ROWS)
            q = q_ref[mp, qrows, :]
            s = lax.dot_general(q, km_ref[mp], NT_DIMS, preferred_element_type=F32) + meta_mask
            sm_s[mp, qrows, :] = s
            mrun = s
            for c in range(n_chunks):
                krows = slice(c * DIFF_TK, (c + 1) * DIFF_TK)
                s = lax.dot_general(q, k_ref[mp, krows, :], NT_DIMS, preferred_element_type=F32)
                s_s[mp, c, qrows, :] = s
                tiles = [s[:, t * HEAD_DIM:(t + 1) * HEAD_DIM] for t in range(lane_tiles)]
                mrun = jnp.maximum(mrun, functools.reduce(jnp.maximum, tiles))
            m_s[mp, qrows, :] = jnp.broadcast_to(jnp.max(mrun, axis=1, keepdims=True),
                                                 (DIFF_STREAM_ROWS, HEAD_DIM))

    pm = [jnp.exp2(sm_s[mp] - m_s[mp]) for mp in range(2)]
    vm = jnp.concatenate([vm_ref[0], vm_ref[1]], axis=1)
    for mp in range(2):
        l_s[mp] = pm[mp]
    acc_s[...] = jnp.dot(jnp.concatenate(pm, axis=0).astype(BF16), vm, preferred_element_type=F32)

    for c in range(n_chunks):
        krows = slice(c * DIFF_TK, (c + 1) * DIFF_TK)
        v = jnp.concatenate([v_ref[0, krows, :], v_ref[1, krows, :]], axis=1)
        ps = []
        for mp in range(2):
            m_b = m_s[mp]
            lsum = l_s[mp]
            parts = []
            for t in range(lane_tiles):
                p = jnp.exp2(s_s[mp, c, :, t * HEAD_DIM:(t + 1) * HEAD_DIM] - m_b)
                lsum = lsum + p
                parts.append(p.astype(BF16))
            l_s[mp] = lsum
            ps.append(jnp.concatenate(parts, axis=1))
        acc_s[...] += jnp.dot(jnp.concatenate(ps, axis=0), v, preferred_element_type=F32)

    lam = (jnp.exp(jnp.sum(lq1_ref[...] * lk1_ref[...], axis=1, keepdims=True))
           - jnp.exp(jnp.sum(lq2_ref[...] * lk2_ref[...], axis=1, keepdims=True)) + LAM_INIT)
    l1 = jnp.sum(l_s[0], axis=1, keepdims=True)
    l2 = jnp.sum(l_s[1], axis=1, keepdims=True)
    o = acc_s[:tq, :] * (1.0 / l1) - acc_s[tq:, :] * (lam / l2)
    ms = jnp.mean(o * o, axis=-1, keepdims=True)
    o_ref[...] = ((o * lax.rsqrt(ms + SUBLN_EPS) * g_ref[...]) * (1.0 - LAM_INIT)).astype(o_ref.dtype)


def _diff_attention(p3, pm3, lq1, lk1, lq2, lk2, gain, row0, seq_len, n_seq, tq):
    q0 = (3 * NA_WIDTH) // (2 * HEAD_DIM)
    k0 = q0 + DIFF_HEADS
    v0 = k0 + DIFF_HEADS
    qt_per_seq = seq_len // tq
    n_chunks = seq_len // DIFF_TK

    def kv_spec(chunk0):
        return pl.BlockSpec((2, seq_len, HEAD_DIM),
                            lambda h, qt: (chunk0 + h, row0 // seq_len + qt // qt_per_seq, 0))

    def meta_spec(chunk0):
        return pl.BlockSpec((2, META_PAD, HEAD_DIM), lambda h, qt: (chunk0 + h, 0, 0))

    vec_spec = pl.BlockSpec((1, HEAD_DIM), lambda h, qt: (0, 0))
    return pl.pallas_call(
        _diff_kernel,
        grid=(DIFF_HEADS, n_seq * qt_per_seq),
        in_specs=[vec_spec, vec_spec, vec_spec, vec_spec,
                  pl.BlockSpec((1, 2 * HEAD_DIM), lambda h, qt: (0, 0)),
                  pl.BlockSpec((2, tq, HEAD_DIM), lambda h, qt: (q0 + h, row0 // tq + qt, 0)),
                  kv_spec(k0), kv_spec(v0), meta_spec(k0), meta_spec(v0)],
        out_specs=pl.BlockSpec((tq, 2 * HEAD_DIM), lambda h, qt: (qt, h)),
        out_shape=jax.ShapeDtypeStruct((n_seq * seq_len, DIFF_WIDTH), BF16),
        scratch_shapes=[pltpu.VMEM((2, n_chunks, tq, DIFF_TK), F32),
                        pltpu.VMEM((2, tq, META_PAD), F32),
                        pltpu.VMEM((2, tq, HEAD_DIM), F32),
                        pltpu.VMEM((2, tq, HEAD_DIM), F32),
                        pltpu.VMEM((2 * tq, 2 * HEAD_DIM), F32)],
        compiler_params=_params(("parallel", "arbitrary"), 52),
        name="diff_attention",
    )(lq1.reshape(1, -1), lk1.reshape(1, -1), lq2.reshape(1, -1), lk2.reshape(1, -1),
      gain.reshape(1, -1), p3, p3, p3, pm3, pm3)


def _out_proj_kernel(na_ref, dp_ref, ds_ref, w_ref, r_ref, o_ref, *, prompt_tiles):
    def body(d_ref):
        y = jnp.dot(na_ref[...], w_ref[:NA_WIDTH, :], preferred_element_type=F32)
        y += jnp.dot(d_ref[...], w_ref[NA_WIDTH:, :], preferred_element_type=F32)
        o_ref[...] = r_ref[...] + y

    i = pl.program_id(0)
    pl.when(i < prompt_tiles)(lambda: body(dp_ref))
    pl.when(i >= prompt_tiles)(lambda: body(ds_ref))


def _out_proj(na, d_prompt, d_sample, w, res, tm, tn):
    m = na.shape[0]
    n = w.shape[1]
    pt = d_prompt.shape[0] // tm
    return pl.pallas_call(
        functools.partial(_out_proj_kernel, prompt_tiles=pt),
        grid=(m // tm, n // tn),
        in_specs=[pl.BlockSpec((tm, NA_WIDTH), lambda i, j: (i, 0)),
                  pl.BlockSpec((tm, DIFF_WIDTH), lambda i, j: (jnp.minimum(i, pt - 1), 0)),
                  pl.BlockSpec((tm, DIFF_WIDTH), lambda i, j: (jnp.maximum(i - pt, 0), 0)),
                  pl.BlockSpec((NA_WIDTH + DIFF_WIDTH, tn), lambda i, j: (0, j)),
                  pl.BlockSpec((tm, tn), lambda i, j: (i, j))],
        out_specs=pl.BlockSpec((tm, tn), lambda i, j: (i, j)),
        out_shape=jax.ShapeDtypeStruct((m, n), F32),
        compiler_params=_params(("parallel", "arbitrary"), 52),
        name="out_proj_res",
    )(na, d_prompt, d_sample, w, res)


def kernel(x_prompt, x_sample, meta_tokens, ffn1_norm, ffn1_wg, ffn1_wu, ffn1_wd, mix_norm, w_in, na_rpb, na_norm, diff_lq1, diff_lk1, diff_lq2, diff_lk2, diff_norm, w_out, ffn2_norm, ffn2_wg, ffn2_wu, ffn2_wd, final_norm):
    xp = x_prompt.reshape(PROMPT_LEN, D_MODEL)
    xs = x_sample.reshape(N_SAMPLES * SAMPLE_LEN, D_MODEL)
    meta = jnp.pad(meta_tokens, ((0, META_PAD - N_META), (0, 0)))

    wg1, wu1, wd1 = ffn1_wg[0].astype(BF16), ffn1_wu[0].astype(BF16), ffn1_wd[0].astype(BF16)
    wg2, wu2, wd2 = ffn2_wg[0].astype(BF16), ffn2_wu[0].astype(BF16), ffn2_wd[0].astype(BF16)
    w_in_b = w_in[0].astype(BF16)
    w_out_b = w_out[0].astype(BF16)

    col = jnp.arange(IN_WIDTH)
    is_q_na = col < NA_WIDTH
    is_q_diff = jnp.logical_and(col >= 3 * NA_WIDTH, col < 3 * NA_WIDTH + DIFF_WIDTH)
    col_scale = jnp.where(is_q_na, ATTN_SCALE, jnp.where(is_q_diff, ATTN_SCALE * LOG2E, 1.0))
    col_scale = col_scale.astype(F32).reshape(1, IN_WIDTH)

    pos_real = jnp.concatenate([jnp.arange(PROMPT_LEN, dtype=jnp.int32)]
                               + [jnp.arange(SAMPLE_LEN, dtype=jnp.int32)] * N_SAMPLES) + N_META
    pos_meta = jnp.arange(META_PAD, dtype=jnp.int32)

    xn1, x = _rmsnorm_concat(xp, xs, ffn1_norm[0], 256)
    h1 = _swiglu_res(xn1, x, wg1, wu1, wd1, 1024, 768)
    xnm1 = _rmsnorm(meta, ffn1_norm[0], BF16, META_PAD)
    h1m = _swiglu_res(xnm1, meta, wg1, wu1, wd1, META_PAD, META_PAD)

    xn = _rmsnorm(h1, mix_norm[0], BF16, 256)
    xnm = _rmsnorm(h1m, mix_norm[0], BF16, META_PAD)
    p3 = _in_proj(xn, w_in_b, col_scale, _rope_tables(pos_real), 1024, 1024)
    pm3 = _in_proj(xnm, w_in_b, col_scale, _rope_tables(pos_meta), META_PAD, 1024)
    bias = _na_bias(na_rpb[0])
    na = _na_attention(p3, pm3, bias, na_norm[0])
    lam_args = (diff_lq1[0], diff_lk1[0], diff_lq2[0], diff_lk2[0], diff_norm[0])
    dd_prompt = _diff_attention(p3, pm3, *lam_args, row0=0, seq_len=PROMPT_LEN, n_seq=1, tq=256)
    dd_sample = _diff_attention(p3, pm3, *lam_args, row0=PROMPT_LEN, seq_len=SAMPLE_LEN, n_seq=N_SAMPLES, tq=512)
    h2 = _out_proj(na, dd_prompt, dd_sample, w_out_b, h1, 1024, 512)

    xn2 = _rmsnorm(h2, ffn2_norm[0], BF16, 256)
    h3 = _swiglu_res(xn2, h2, wg2, wu2, wd2, 1024, 768)
    y_prompt, y_sample = _rmsnorm_split(h3, final_norm, PROMPT_LEN, 256)
    return (y_prompt.reshape(1, PROMPT_LEN, D_MODEL), y_sample.reshape(N_SAMPLES, SAMPLE_LEN, D_MODEL))
```

```python
import functools
import math

import jax
import jax.numpy as jnp
from jax import lax
from jax.experimental import pallas as pl
from jax.experimental.pallas import tpu as pltpu

F32 = jnp.float32
BF16 = jnp.bfloat16

D_MODEL = 4096
N_META = 16
META_PAD = 128
GRID_W = 64
WIN_H = 8
WIN_W = 16
NA_HEADS = 16
HEAD_DIM = 128
NA_WIDTH = NA_HEADS * HEAD_DIM
DIFF_HEADS = 8
DIFF_WIDTH = DIFF_HEADS * 2 * HEAD_DIM
IN_WIDTH = 3 * NA_WIDTH + 3 * DIFF_WIDTH
N_CHUNKS = IN_WIDTH // HEAD_DIM
D_FF = 11008
ROPE_THETA = 500000.0
ROPE_DIM = HEAD_DIM // 4
EPS = 1e-6
SUBLN_EPS = 1e-5
LAM_INIT = 0.8 - 0.6 * math.exp(-0.3 * 0)
ATTN_SCALE = HEAD_DIM ** -0.5
LOG2E = math.log2(math.e)
NEG = -1e30

PROMPT_LEN = 8192
SAMPLE_LEN = 4096
N_SAMPLES = 4
T_REAL = PROMPT_LEN + N_SAMPLES * SAMPLE_LEN

NA_Q = 256
NA_ROWS_Q = NA_Q // GRID_W
NA_KEYS = 3 * NA_Q
NA_GROUPS = T_REAL // NA_Q
NA_HEADS_PER_STEP = 8
NA_PROMPT_GROUPS = PROMPT_LEN // NA_Q
NA_SAMPLE_GROUPS = SAMPLE_LEN // NA_Q

DIFF_TK = 1024
DIFF_STREAM_ROWS = 128

MIB = 1024 * 1024
NT_DIMS = (((1,), (1,)), ((), ()))
PROJ_SUB = 256
DOWN_TN = 256
OUT_TN = 512


def _params(semantics, vmem_mib):
    return pltpu.CompilerParams(dimension_semantics=semantics, vmem_limit_bytes=vmem_mib * MIB)


def _rmsnorm_kernel(x_ref, g_ref, o_ref, *, eps):
    x = x_ref[...]
    ms = jnp.mean(x * x, axis=-1, keepdims=True)
    o_ref[...] = (x * lax.rsqrt(ms + eps) * g_ref[...]).astype(o_ref.dtype)


def _rmsnorm(x, gain, out_dtype, tm):
    m, d = x.shape
    return pl.pallas_call(
        functools.partial(_rmsnorm_kernel, eps=EPS),
        grid=(m // tm,),
        in_specs=[pl.BlockSpec((tm, d), lambda i: (i, 0)),
                  pl.BlockSpec((1, d), lambda i: (0, 0))],
        out_specs=pl.BlockSpec((tm, d), lambda i: (i, 0)),
        out_shape=jax.ShapeDtypeStruct((m, d), out_dtype),
        compiler_params=_params(("parallel",), 32),
        name="rmsnorm",
    )(x, gain.reshape(1, d))


def _rmsnorm_concat_kernel(xp_ref, xs_ref, g_ref, o_ref, x_ref, *, eps, prompt_tiles):
    def emit(src_ref):
        x = src_ref[...]
        ms = jnp.mean(x * x, axis=-1, keepdims=True)
        o_ref[...] = (x * lax.rsqrt(ms + eps) * g_ref[...]).astype(o_ref.dtype)
        x_ref[...] = x

    i = pl.program_id(0)
    pl.when(i < prompt_tiles)(lambda: emit(xp_ref))
    pl.when(i >= prompt_tiles)(lambda: emit(xs_ref))


def _rmsnorm_concat(xp, xs, gain, tm):
    d = xp.shape[1]
    pt = xp.shape[0] // tm
    m = xp.shape[0] + xs.shape[0]
    row_spec = pl.BlockSpec((tm, d), lambda i: (i, 0))
    return pl.pallas_call(
        functools.partial(_rmsnorm_concat_kernel, eps=EPS, prompt_tiles=pt),
        grid=(m // tm,),
        in_specs=[pl.BlockSpec((tm, d), lambda i: (jnp.minimum(i, pt - 1), 0)),
                  pl.BlockSpec((tm, d), lambda i: (jnp.maximum(i - pt, 0), 0)),
                  pl.BlockSpec((1, d), lambda i: (0, 0))],
        out_specs=[row_spec, row_spec],
        out_shape=[jax.ShapeDtypeStruct((m, d), BF16), jax.ShapeDtypeStruct((m, d), F32)],
        compiler_params=_params(("arbitrary",), 40),
        name="rmsnorm_concat",
    )(xp, xs, gain.reshape(1, d))


def _rmsnorm_split_kernel(x_ref, g_ref, op_ref, os_ref, *, eps, prompt_tiles):
    x = x_ref[...]
    ms = jnp.mean(x * x, axis=-1, keepdims=True)
    y = x * lax.rsqrt(ms + eps) * g_ref[...]
    i = pl.program_id(0)

    @pl.when(i < prompt_tiles)
    def _():
        op_ref[...] = y

    @pl.when(i >= prompt_tiles)
    def _():
        os_ref[...] = y


def _rmsnorm_split(x, gain, n_prompt, tm):
    m, d = x.shape
    pt = n_prompt // tm
    return pl.pallas_call(
        functools.partial(_rmsnorm_split_kernel, eps=EPS, prompt_tiles=pt),
        grid=(m // tm,),
        in_specs=[pl.BlockSpec((tm, d), lambda i: (i, 0)),
                  pl.BlockSpec((1, d), lambda i: (0, 0))],
        out_specs=[pl.BlockSpec((tm, d), lambda i: (jnp.minimum(i, pt - 1), 0)),
                   pl.BlockSpec((tm, d), lambda i: (jnp.maximum(i - pt, 0), 0))],
        out_shape=[jax.ShapeDtypeStruct((n_prompt, d), F32), jax.ShapeDtypeStruct((m - n_prompt, d), F32)],
        compiler_params=_params(("arbitrary",), 40),
        name="rmsnorm_split",
    )(x, gain.reshape(1, d))


def _gate_up_kernel(x_ref, wg_ref, wu_ref, o_ref):
    x = x_ref[...]
    g = jnp.dot(x, wg_ref[...], preferred_element_type=F32)
    u = jnp.dot(x, wu_ref[...], preferred_element_type=F32)
    o_ref[...] = (0.5 * jax.nn.silu(g) * u).astype(o_ref.dtype)


def _gate_up(xn, wg, wu, tm, tn):
    m, d = xn.shape
    f = wg.shape[1]
    return pl.pallas_call(
        _gate_up_kernel,
        grid=(m // tm, pl.cdiv(f, tn)),
        in_specs=[pl.BlockSpec((tm, d), lambda i, j: (i, 0)),
                  pl.BlockSpec((d, tn), lambda i, j: (0, j)),
                  pl.BlockSpec((d, tn), lambda i, j: (0, j))],
        out_specs=pl.BlockSpec((tm, tn), lambda i, j: (i, j)),
        out_shape=jax.ShapeDtypeStruct((m, f), BF16),
        compiler_params=_params(("parallel", "arbitrary"), 52),
        name="ffn_gate_up",
    )(xn, wg, wu)


def _down_res_kernel(a_ref, w_ref, r_ref, o_ref):
    o_ref[...] = r_ref[...] + jnp.dot(a_ref[...], w_ref[0], preferred_element_type=F32)


def _column_blocks(w, tn):
    k, n = w.shape
    return w.reshape(k, n // tn, tn).transpose(1, 0, 2)


def _down_res(a, w_blocks, res, tm):
    m, f = a.shape
    nb, _, tn = w_blocks.shape
    d = nb * tn
    return pl.pallas_call(
        _down_res_kernel,
        grid=(m // tm, nb),
        in_specs=[pl.BlockSpec((tm, f), lambda i, j: (i, 0)),
                  pl.BlockSpec((1, f, tn), lambda i, j: (j, 0, 0)),
                  pl.BlockSpec((tm, tn), lambda i, j: (i, j))],
        out_specs=pl.BlockSpec((tm, tn), lambda i, j: (i, j)),
        out_shape=jax.ShapeDtypeStruct((m, d), F32),
        compiler_params=_params(("parallel", "arbitrary"), 56),
        name="ffn_down_res",
    )(a, w_blocks, res)


def _swiglu_res(xn, res, wg, wu, wd_blocks, tm_up, tm_down):
    a = _gate_up(xn, wg, wu, tm_up, 512)
    return _down_res(a, wd_blocks, res, tm_down)


def _in_proj_kernel(x_ref, w_ref, cs_ref, cos_ref, sa_ref, sb_ref, o_ref, *, rope_lo, rope_hi):
    j = pl.program_id(1)
    tn = w_ref.shape[1]
    is_rope = jnp.logical_and(j >= rope_lo, j < rope_hi)

    def body(rope):
        x = x_ref[...]
        for sub in range(tn // PROJ_SUB):
            cols = slice(sub * PROJ_SUB, (sub + 1) * PROJ_SUB)
            y = jnp.dot(x, w_ref[:, cols], preferred_element_type=F32) * cs_ref[:, cols]
            for c in range(PROJ_SUB // HEAD_DIM):
                yc = y[:, c * HEAD_DIM:(c + 1) * HEAD_DIM]
                if rope:
                    upper = pltpu.roll(yc, HEAD_DIM - ROPE_DIM // 2, 1)
                    lower = pltpu.roll(yc, ROPE_DIM // 2, 1)
                    yc = yc * cos_ref[...] + upper * sa_ref[...] + lower * sb_ref[...]
                o_ref[sub * (PROJ_SUB // HEAD_DIM) + c] = yc.astype(o_ref.dtype)

    pl.when(is_rope)(lambda: body(True))
    pl.when(jnp.logical_not(is_rope))(lambda: body(False))


def _in_proj(xn, w, col_scale, rope_tabs, tm, tn):
    m, d = xn.shape
    n = w.shape[1]
    cpb = tn // HEAD_DIM
    rope_lo = (3 * NA_WIDTH) // tn
    rope_hi = (3 * NA_WIDTH + 2 * DIFF_WIDTH) // tn
    tab_spec = pl.BlockSpec((tm, HEAD_DIM), lambda i, j: (i, 0))
    return pl.pallas_call(
        functools.partial(_in_proj_kernel, rope_lo=rope_lo, rope_hi=rope_hi),
        grid=(m // tm, n // tn),
        in_specs=[pl.BlockSpec((tm, d), lambda i, j: (i, 0)),
                  pl.BlockSpec((d, tn), lambda i, j: (0, j)),
                  pl.BlockSpec((1, tn), lambda i, j: (0, j)),
                  tab_spec, tab_spec, tab_spec],
        out_specs=pl.BlockSpec((cpb, tm, HEAD_DIM), lambda i, j: (j, i, 0)),
        out_shape=jax.ShapeDtypeStruct((n // HEAD_DIM, m, HEAD_DIM), BF16),
        compiler_params=_params(("parallel", "arbitrary"), 56),
        name="in_proj_rope",
    )(xn, w, col_scale, *rope_tabs)


def _rope_tables(pos):
    half = ROPE_DIM // 2
    inv = ROPE_THETA ** (-2.0 * jnp.arange(half, dtype=F32) / ROPE_DIM)
    ang = pos.astype(F32)[:, None] * inv[None, :]
    cos, sin = jnp.cos(ang), jnp.sin(ang)
    n = pos.shape[0]
    ones = jnp.ones((n, HEAD_DIM - ROPE_DIM), F32)
    zeros_h = jnp.zeros((n, half), F32)
    zeros_r = jnp.zeros((n, HEAD_DIM - ROPE_DIM), F32)
    cos_t = jnp.concatenate([cos, cos, ones], axis=1)
    sa_t = jnp.concatenate([-sin, zeros_h, zeros_r], axis=1)
    sb_t = jnp.concatenate([zeros_h, sin, zeros_r], axis=1)
    return cos_t, sa_t, sb_t


def _na_bias_kernel(rpb_ref, o_ref):
    h = pl.program_id(0)
    shape = (GRID_W, 2 * GRID_W)
    c = lax.broadcasted_iota(jnp.int32, shape, 0)
    lane = lax.broadcasted_iota(jnp.int32, shape, 1)
    second = lane >= GRID_W
    cc = jnp.bitwise_and(lane, GRID_W - 1)
    rel = cc - c + (WIN_W - 1)
    c0 = jnp.clip(c - WIN_W // 2, 0, GRID_W - WIN_W)
    col_ok = jnp.logical_and(cc >= c0, cc < c0 + WIN_W)
    n_dr = 2 * WIN_H - 1
    n_dc = 2 * WIN_W - 1
    base = h * (n_dr * n_dc)

    pair = []
    for d in range(n_dr - 1):
        val = jnp.zeros(shape, F32)
        for k in range(n_dc):
            w0 = rpb_ref[base + d * n_dc + k]
            w1 = rpb_ref[base + (d + 1) * n_dc + k]
            val = jnp.where(rel == k, jnp.where(second, w1, w0), val)
        pair.append(jnp.where(col_ok, val, NEG))

    key_rows = 3 * NA_ROWS_Q
    row_of_lane = second.astype(jnp.int32)
    for qr in range(NA_ROWS_Q):
        for jp in range(key_rows // 2):
            j = 2 * jp + row_of_lane
            tile = pair[2 * jp + (NA_ROWS_Q - 1) - qr]
            rows = pl.ds(qr * GRID_W, GRID_W)
            cols = pl.ds(jp * 2 * GRID_W, 2 * GRID_W)
            first_ok = j >= NA_ROWS_Q
            mid_ok = jnp.logical_and(j >= qr, j <= qr + WIN_H - 1)
            last_ok = j <= WIN_H - 1
            o_ref[0, 0, rows, cols] = jnp.where(first_ok, tile, NEG)
            o_ref[1, 0, rows, cols] = jnp.where(mid_ok, tile, NEG)
            o_ref[2, 0, rows, cols] = jnp.where(last_ok, tile, NEG)


def _na_bias(rpb):
    return pl.pallas_call(
        _na_bias_kernel,
        grid=(NA_HEADS,),
        in_specs=[pl.BlockSpec(memory_space=pltpu.SMEM)],
        out_specs=pl.BlockSpec((3, 1, NA_Q, NA_KEYS), lambda h: (0, h, 0, 0)),
        out_shape=jax.ShapeDtypeStruct((3, NA_HEADS, NA_Q, NA_KEYS), F32),
        compiler_params=_params(("arbitrary",), 32),
        name="na_bias_table",
    )(rpb.reshape(-1))


def _na_kernel(q_ref, k0_ref, k1_ref, k2_ref, v0_ref, v1_ref, v2_ref, km_ref, vm_ref, b_ref, g_ref, o_ref):
    lane = lax.broadcasted_iota(jnp.int32, (1, META_PAD), 1)
    meta_mask = jnp.where(lane < N_META, 0.0, NEG).astype(F32)

    for hh in range(q_ref.shape[0]):
        q = q_ref[hh]
        s = [lax.dot_general(q, k_ref[hh], NT_DIMS, preferred_element_type=F32)
             + b_ref[0, hh, :, j * NA_Q:(j + 1) * NA_Q]
             for j, k_ref in enumerate((k0_ref, k1_ref, k2_ref))]
        s.append(lax.dot_general(q, km_ref[hh], NT_DIMS, preferred_element_type=F32) + meta_mask)

        mx = functools.reduce(jnp.maximum, [jnp.max(x, axis=1, keepdims=True) for x in s])
        p = [jnp.exp(x - mx) for x in s]
        denom = functools.reduce(jnp.add, [jnp.sum(x, axis=1, keepdims=True) for x in p])
        vals = (v0_ref[hh], v1_ref[hh], v2_ref[hh], vm_ref[hh])
        o = functools.reduce(jnp.add, [jnp.dot(x.astype(BF16), v, preferred_element_type=F32)
                                       for x, v in zip(p, vals)])
        o = o * (1.0 / denom)
        ms = jnp.mean(o * o, axis=-1, keepdims=True)
        o_ref[:, hh * HEAD_DIM:(hh + 1) * HEAD_DIM] = (o * lax.rsqrt(ms + EPS) * g_ref[hh]).astype(o_ref.dtype)


def _na_edge_type(g):
    local = jnp.where(g < NA_PROMPT_GROUPS, g, jnp.bitwise_and(g - NA_PROMPT_GROUPS, NA_SAMPLE_GROUPS - 1))
    last = jnp.where(g < NA_PROMPT_GROUPS, NA_PROMPT_GROUPS - 1, NA_SAMPLE_GROUPS - 1)
    return jnp.where(local == 0, 0, jnp.where(local == last, 2, 1))


def _na_attention(p3, pm3, bias, gain):
    hps = NA_HEADS_PER_STEP
    steps_per_group = NA_HEADS // hps

    def tok_spec(block0, shift):
        def index(hb, g):
            return (block0 + hb, jnp.clip(g + shift, 0, NA_GROUPS - 1), 0)
        return pl.BlockSpec((hps, NA_Q, HEAD_DIM), index)

    def meta_spec(block0):
        return pl.BlockSpec((hps, META_PAD, HEAD_DIM), lambda hb, g: (block0 + hb, 0, 0))

    k0, v0 = steps_per_group, 2 * steps_per_group
    return pl.pallas_call(
        _na_kernel,
        grid=(steps_per_group, NA_GROUPS),
        in_specs=[tok_spec(0, 0),
                  tok_spec(k0, -1), tok_spec(k0, 0), tok_spec(k0, 1),
                  tok_spec(v0, -1), tok_spec(v0, 0), tok_spec(v0, 1),
                  meta_spec(k0), meta_spec(v0),
                  pl.BlockSpec((1, hps, NA_Q, NA_KEYS), lambda hb, g: (_na_edge_type(g), hb, 0, 0)),
                  pl.BlockSpec((hps, 1, HEAD_DIM), lambda hb, g: (hb, 0, 0))],
        out_specs=pl.BlockSpec((NA_Q, hps * HEAD_DIM), lambda hb, g: (g, hb)),
        out_shape=jax.ShapeDtypeStruct((T_REAL, NA_WIDTH), BF16),
        compiler_params=_params(("parallel", "arbitrary"), 40),
        name="na_attention",
    )(p3, p3, p3, p3, p3, p3, p3, pm3, pm3, bias, gain.reshape(NA_HEADS, 1, HEAD_DIM))


def _diff_kernel(lq1_ref, lk1_ref, lq2_ref, lk2_ref, g_ref, q_ref, k_ref, v_ref,
                 km_ref, vm_ref, o_ref, s_s, sm_s, m_s, l_s, acc_s):
    tq = q_ref.shape[1]
    n_chunks = k_ref.shape[1] // DIFF_TK
    lane_tiles = DIFF_TK // HEAD_DIM
    lane = lax.broadcasted_iota(jnp.int32, (1, META_PAD), 1)
    meta_mask = jnp.where(lane < N_META, 0.0, NEG).astype(F32)

    for mp in range(2):
        for r0 in range(0, tq, DIFF_STREAM_ROWS):
            qrows = slice(r0, r0 + DIFF_STREAM_ROWS)
            q = q_ref[mp, qrows, :]
            s = lax.dot_general(q, km_ref[mp], NT_DIMS, preferred_element_type=F32) + meta_mask
            sm_s[mp, qrows, :] = s
            mrun = s
            for c in range(n_chunks):
                krows = slice(c * DIFF_TK, (c + 1) * DIFF_TK)
                s = lax.dot_general(q, k_ref[mp, krows, :], NT_DIMS, preferred_element_type=F32)
                s_s[mp, c, qrows, :] = s
                tiles = [s[:, t * HEAD_DIM:(t + 1) * HEAD_DIM] for t in range(lane_tiles)]
                mrun = jnp.maximum(mrun, functools.reduce(jnp.maximum, tiles))
            m_s[mp, qrows, :] = jnp.broadcast_to(jnp.max(mrun, axis=1, keepdims=True),
                                                 (DIFF_STREAM_ROWS, HEAD_DIM))

    pm = [jnp.exp2(sm_s[mp] - m_s[mp]) for mp in range(2)]
    vm = jnp.concatenate([vm_ref[0], vm_ref[1]], axis=1)
    for mp in range(2):
        l_s[mp] = pm[mp]
    acc_s[...] = jnp.dot(jnp.concatenate(pm, axis=0).astype(BF16), vm, preferred_element_type=F32)

    for c in range(n_chunks):
        krows = slice(c * DIFF_TK, (c + 1) * DIFF_TK)
        v = jnp.concatenate([v_ref[0, krows, :], v_ref[1, krows, :]], axis=1)
        ps = []
        for mp in range(2):
            m_b = m_s[mp]
            lsum = l_s[mp]
            parts = []
            for t in range(lane_tiles):
                p = jnp.exp2(s_s[mp, c, :, t * HEAD_DIM:(t + 1) * HEAD_DIM] - m_b)
                lsum = lsum + p
                parts.append(p.astype(BF16))
            l_s[mp] = lsum
            ps.append(jnp.concatenate(parts, axis=1))
        acc_s[...] += jnp.dot(jnp.concatenate(ps, axis=0), v, preferred_element_type=F32)

    lam = (jnp.exp(jnp.sum(lq1_ref[...] * lk1_ref[...], axis=1, keepdims=True))
           - jnp.exp(jnp.sum(lq2_ref[...] * lk2_ref[...], axis=1, keepdims=True)) + LAM_INIT)
    l1 = jnp.sum(l_s[0], axis=1, keepdims=True)
    l2 = jnp.sum(l_s[1], axis=1, keepdims=True)
    o = acc_s[:tq, :] * (1.0 / l1) - acc_s[tq:, :] * (lam / l2)
    ms = jnp.mean(o * o, axis=-1, keepdims=True)
    o_ref[...] = ((o * lax.rsqrt(ms + SUBLN_EPS) * g_ref[...]) * (1.0 - LAM_INIT)).astype(o_ref.dtype)


def _diff_attention(p3, pm3, lq1, lk1, lq2, lk2, gain, row0, seq_len, n_seq, tq):
    q0 = (3 * NA_WIDTH) // (2 * HEAD_DIM)
    k0 = q0 + DIFF_HEADS
    v0 = k0 + DIFF_HEADS
    qt_per_seq = seq_len // tq
    n_chunks = seq_len // DIFF_TK

    def kv_spec(chunk0):
        return pl.BlockSpec((2, seq_len, HEAD_DIM),
                            lambda h, qt: (chunk0 + h, row0 // seq_len + qt // qt_per_seq, 0))

    def meta_spec(chunk0):
        return pl.BlockSpec((2, META_PAD, HEAD_DIM), lambda h, qt: (chunk0 + h, 0, 0))

    vec_spec = pl.BlockSpec((1, HEAD_DIM), lambda h, qt: (0, 0))
    return pl.pallas_call(
        _diff_kernel,
        grid=(DIFF_HEADS, n_seq * qt_per_seq),
        in_specs=[vec_spec, vec_spec, vec_spec, vec_spec,
                  pl.BlockSpec((1, 2 * HEAD_DIM), lambda h, qt: (0, 0)),
                  pl.BlockSpec((2, tq, HEAD_DIM), lambda h, qt: (q0 + h, row0 // tq + qt, 0)),
                  kv_spec(k0), kv_spec(v0), meta_spec(k0), meta_spec(v0)],
        out_specs=pl.BlockSpec((tq, 2 * HEAD_DIM), lambda h, qt: (qt, h)),
        out_shape=jax.ShapeDtypeStruct((n_seq * seq_len, DIFF_WIDTH), BF16),
        scratch_shapes=[pltpu.VMEM((2, n_chunks, tq, DIFF_TK), F32),
                        pltpu.VMEM((2, tq, META_PAD), F32),
                        pltpu.VMEM((2, tq, HEAD_DIM), F32),
                        pltpu.VMEM((2, tq, HEAD_DIM), F32),
                        pltpu.VMEM((2 * tq, 2 * HEAD_DIM), F32)],
        compiler_params=_params(("parallel", "arbitrary"), 52),
        name="diff_attention",
    )(lq1.reshape(1, -1), lk1.reshape(1, -1), lq2.reshape(1, -1), lk2.reshape(1, -1),
      gain.reshape(1, -1), p3, p3, p3, pm3, pm3)


def _out_proj_kernel(na_ref, dp_ref, ds_ref, w_ref, r_ref, o_ref, *, prompt_tiles):
    def body(d_ref):
        y = jnp.dot(na_ref[...], w_ref[0, :NA_WIDTH, :], preferred_element_type=F32)
        y += jnp.dot(d_ref[...], w_ref[0, NA_WIDTH:, :], preferred_element_type=F32)
        o_ref[...] = r_ref[...] + y

    i = pl.program_id(0)
    pl.when(i < prompt_tiles)(lambda: body(dp_ref))
    pl.when(i >= prompt_tiles)(lambda: body(ds_ref))


def _out_proj(na, d_prompt, d_sample, w_blocks, res, tm):
    m = na.shape[0]
    nb, _, tn = w_blocks.shape
    n = nb * tn
    pt = d_prompt.shape[0] // tm
    return pl.pallas_call(
        functools.partial(_out_proj_kernel, prompt_tiles=pt),
        grid=(m // tm, nb),
        in_specs=[pl.BlockSpec((tm, NA_WIDTH), lambda i, j: (i, 0)),
                  pl.BlockSpec((tm, DIFF_WIDTH), lambda i, j: (jnp.minimum(i, pt - 1), 0)),
                  pl.BlockSpec((tm, DIFF_WIDTH), lambda i, j: (jnp.maximum(i - pt, 0), 0)),
                  pl.BlockSpec((1, NA_WIDTH + DIFF_WIDTH, tn), lambda i, j: (j, 0, 0)),
                  pl.BlockSpec((tm, tn), lambda i, j: (i, j))],
        out_specs=pl.BlockSpec((tm, tn), lambda i, j: (i, j)),
        out_shape=jax.ShapeDtypeStruct((m, n), F32),
        compiler_params=_params(("parallel", "arbitrary"), 52),
        name="out_proj_res",
    )(na, d_prompt, d_sample, w_blocks, res)


def kernel(x_prompt, x_sample, meta_tokens, ffn1_norm, ffn1_wg, ffn1_wu, ffn1_wd, mix_norm, w_in, na_rpb, na_norm, diff_lq1, diff_lk1, diff_lq2, diff_lk2, diff_norm, w_out, ffn2_norm, ffn2_wg, ffn2_wu, ffn2_wd, final_norm):
    xp = x_prompt.reshape(PROMPT_LEN, D_MODEL)
    xs = x_sample.reshape(N_SAMPLES * SAMPLE_LEN, D_MODEL)
    meta = jnp.pad(meta_tokens, ((0, META_PAD - N_META), (0, 0)))

    wg1, wu1 = ffn1_wg[0].astype(BF16), ffn1_wu[0].astype(BF16)
    wg2, wu2 = ffn2_wg[0].astype(BF16), ffn2_wu[0].astype(BF16)
    wd1 = _column_blocks(ffn1_wd[0].astype(BF16), DOWN_TN)
    wd2 = _column_blocks(ffn2_wd[0].astype(BF16), DOWN_TN)
    w_in_b = w_in[0].astype(BF16)
    w_out_b = _column_blocks(w_out[0].astype(BF16), OUT_TN)

    col = jnp.arange(IN_WIDTH)
    is_q_na = col < NA_WIDTH
    is_q_diff = jnp.logical_and(col >= 3 * NA_WIDTH, col < 3 * NA_WIDTH + DIFF_WIDTH)
    col_scale = jnp.where(is_q_na, ATTN_SCALE, jnp.where(is_q_diff, ATTN_SCALE * LOG2E, 1.0))
    col_scale = col_scale.astype(F32).reshape(1, IN_WIDTH)

    pos_real = jnp.concatenate([jnp.arange(PROMPT_LEN, dtype=jnp.int32)]
                               + [jnp.arange(SAMPLE_LEN, dtype=jnp.int32)] * N_SAMPLES) + N_META
    pos_meta = jnp.arange(META_PAD, dtype=jnp.int32)

    xn1, x = _rmsnorm_concat(xp, xs, ffn1_norm[0], 256)
    h1 = _swiglu_res(xn1, x, wg1, wu1, wd1, 1024, 768)
    xnm1 = _rmsnorm(meta, ffn1_norm[0], BF16, META_PAD)
    h1m = _swiglu_res(xnm1, meta, wg1, wu1, wd1, META_PAD, META_PAD)

    xn = _rmsnorm(h1, mix_norm[0], BF16, 256)
    xnm = _rmsnorm(h1m, mix_norm[0], BF16, META_PAD)
    p3 = _in_proj(xn, w_in_b, col_scale, _rope_tables(pos_real), 1024, 1024)
    pm3 = _in_proj(xnm, w_in_b, col_scale, _rope_tables(pos_meta), META_PAD, 1024)
    bias = _na_bias(na_rpb[0])
    na = _na_attention(p3, pm3, bias, na_norm[0])
    lam_args = (diff_lq1[0], diff_lk1[0], diff_lq2[0], diff_lk2[0], diff_norm[0])
    dd_prompt = _diff_attention(p3, pm3, *lam_args, row0=0, seq_len=PROMPT_LEN, n_seq=1, tq=256)
    dd_sample = _diff_attention(p3, pm3, *lam_args, row0=PROMPT_LEN, seq_len=SAMPLE_LEN, n_seq=N_SAMPLES, tq=512)
    h2 = _out_proj(na, dd_prompt, dd_sample, w_out_b, h1, 1024)

    xn2 = _rmsnorm(h2, ffn2_norm[0], BF16, 256)
    h3 = _swiglu_res(xn2, h2, wg2, wu2, wd2, 1024, 768)
    y_prompt, y_sample = _rmsnorm_split(h3, final_norm, PROMPT_LEN, 256)
    return (y_prompt.reshape(1, PROMPT_LEN, D_MODEL), y_sample.reshape(N_SAMPLES, SAMPLE_LEN, D_MODEL))
```

```python
import functools
import math

import jax
import jax.numpy as jnp
from jax import lax
from jax.experimental import pallas as pl
from jax.experimental.pallas import tpu as pltpu

F32 = jnp.float32
BF16 = jnp.bfloat16

D_MODEL = 4096
N_META = 16
META_PAD = 128
GRID_W = 64
WIN_H = 8
WIN_W = 16
NA_HEADS = 16
HEAD_DIM = 128
NA_WIDTH = NA_HEADS * HEAD_DIM
DIFF_HEADS = 8
DIFF_WIDTH = DIFF_HEADS * 2 * HEAD_DIM
IN_WIDTH = 3 * NA_WIDTH + 3 * DIFF_WIDTH
N_CHUNKS = IN_WIDTH // HEAD_DIM
D_FF = 11008
ROPE_THETA = 500000.0
ROPE_DIM = HEAD_DIM // 4
EPS = 1e-6
SUBLN_EPS = 1e-5
LAM_INIT = 0.8 - 0.6 * math.exp(-0.3 * 0)
ATTN_SCALE = HEAD_DIM ** -0.5
LOG2E = math.log2(math.e)
NEG = -1e30

PROMPT_LEN = 8192
SAMPLE_LEN = 4096
N_SAMPLES = 4
T_REAL = PROMPT_LEN + N_SAMPLES * SAMPLE_LEN

NA_Q = 256
NA_ROWS_Q = NA_Q // GRID_W
NA_KEYS = 3 * NA_Q
NA_GROUPS = T_REAL // NA_Q
NA_HEADS_PER_STEP = 8
NA_PROMPT_GROUPS = PROMPT_LEN // NA_Q
NA_SAMPLE_GROUPS = SAMPLE_LEN // NA_Q

DIFF_TK = 1024
DIFF_STREAM_ROWS = 128

MIB = 1024 * 1024
NT_DIMS = (((1,), (1,)), ((), ()))
PROJ_SUB = 256


def _params(semantics, vmem_mib):
    return pltpu.CompilerParams(dimension_semantics=semantics, vmem_limit_bytes=vmem_mib * MIB)


def _rmsnorm_kernel(x_ref, g_ref, o_ref, *, eps):
    x = x_ref[...]
    ms = jnp.mean(x * x, axis=-1, keepdims=True)
    o_ref[...] = (x * lax.rsqrt(ms + eps) * g_ref[...]).astype(o_ref.dtype)


def _rmsnorm(x, gain, out_dtype, tm):
    m, d = x.shape
    return pl.pallas_call(
        functools.partial(_rmsnorm_kernel, eps=EPS),
        grid=(m // tm,),
        in_specs=[pl.BlockSpec((tm, d), lambda i: (i, 0)),
                  pl.BlockSpec((1, d), lambda i: (0, 0))],
        out_specs=pl.BlockSpec((tm, d), lambda i: (i, 0)),
        out_shape=jax.ShapeDtypeStruct((m, d), out_dtype),
        compiler_params=_params(("parallel",), 32),
        name="rmsnorm",
    )(x, gain.reshape(1, d))


def _rmsnorm_concat_kernel(xp_ref, xs_ref, g_ref, o_ref, x_ref, *, eps, prompt_tiles):
    def emit(src_ref):
        x = src_ref[...]
        ms = jnp.mean(x * x, axis=-1, keepdims=True)
        o_ref[...] = (x * lax.rsqrt(ms + eps) * g_ref[...]).astype(o_ref.dtype)
        x_ref[...] = x

    i = pl.program_id(0)
    pl.when(i < prompt_tiles)(lambda: emit(xp_ref))
    pl.when(i >= prompt_tiles)(lambda: emit(xs_ref))


def _rmsnorm_concat(xp, xs, gain, tm):
    d = xp.shape[1]
    pt = xp.shape[0] // tm
    m = xp.shape[0] + xs.shape[0]
    row_spec = pl.BlockSpec((tm, d), lambda i: (i, 0))
    return pl.pallas_call(
        functools.partial(_rmsnorm_concat_kernel, eps=EPS, prompt_tiles=pt),
        grid=(m // tm,),
        in_specs=[pl.BlockSpec((tm, d), lambda i: (jnp.minimum(i, pt - 1), 0)),
                  pl.BlockSpec((tm, d), lambda i: (jnp.maximum(i - pt, 0), 0)),
                  pl.BlockSpec((1, d), lambda i: (0, 0))],
        out_specs=[row_spec, row_spec],
        out_shape=[jax.ShapeDtypeStruct((m, d), BF16), jax.ShapeDtypeStruct((m, d), F32)],
        compiler_params=_params(("arbitrary",), 40),
        name="rmsnorm_concat",
    )(xp, xs, gain.reshape(1, d))


def _rmsnorm_split_kernel(x_ref, g_ref, op_ref, os_ref, *, eps, prompt_tiles):
    x = x_ref[...]
    ms = jnp.mean(x * x, axis=-1, keepdims=True)
    y = x * lax.rsqrt(ms + eps) * g_ref[...]
    i = pl.program_id(0)

    @pl.when(i < prompt_tiles)
    def _():
        op_ref[...] = y

    @pl.when(i >= prompt_tiles)
    def _():
        os_ref[...] = y


def _rmsnorm_split(x, gain, n_prompt, tm):
    m, d = x.shape
    pt = n_prompt // tm
    return pl.pallas_call(
        functools.partial(_rmsnorm_split_kernel, eps=EPS, prompt_tiles=pt),
        grid=(m // tm,),
        in_specs=[pl.BlockSpec((tm, d), lambda i: (i, 0)),
                  pl.BlockSpec((1, d), lambda i: (0, 0))],
        out_specs=[pl.BlockSpec((tm, d), lambda i: (jnp.minimum(i, pt - 1), 0)),
                   pl.BlockSpec((tm, d), lambda i: (jnp.maximum(i - pt, 0), 0))],
        out_shape=[jax.ShapeDtypeStruct((n_prompt, d), F32), jax.ShapeDtypeStruct((m - n_prompt, d), F32)],
        compiler_params=_params(("arbitrary",), 40),
        name="rmsnorm_split",
    )(x, gain.reshape(1, d))


def _store_row_rstd(x_ref, rstd_ref):
    x = x_ref[...].astype(F32)
    ms = jnp.mean(x * x, axis=-1, keepdims=True)
    rstd_ref[...] = jnp.broadcast_to(lax.rsqrt(ms + EPS), rstd_ref.shape)


def _gate_up_kernel(x_ref, wg_ref, wu_ref, o_ref, *maybe_rstd_ref):
    for rstd_ref in maybe_rstd_ref:
        pl.when(pl.program_id(1) == 0)(functools.partial(_store_row_rstd, x_ref, rstd_ref))
    x = x_ref[...]
    g = jnp.dot(x, wg_ref[...], preferred_element_type=F32)
    u = jnp.dot(x, wu_ref[...], preferred_element_type=F32)
    for c in range(o_ref.shape[1] // HEAD_DIM):
        cols = slice(c * HEAD_DIM, (c + 1) * HEAD_DIM)
        gc, uc = g[:, cols], u[:, cols]
        for rstd_ref in maybe_rstd_ref:
            gc, uc = gc * rstd_ref[...], uc * rstd_ref[...]
        o_ref[:, cols] = (0.5 * jax.nn.silu(gc) * uc).astype(o_ref.dtype)


def _gate_up(x, wg, wu, tm, tn, normalize):
    m, d = x.shape
    f = wg.shape[1]
    return pl.pallas_call(
        _gate_up_kernel,
        scratch_shapes=[pltpu.VMEM((tm, HEAD_DIM), F32)] if normalize else [],
        grid=(m // tm, pl.cdiv(f, tn)),
        in_specs=[pl.BlockSpec((tm, d), lambda i, j: (i, 0)),
                  pl.BlockSpec((d, tn), lambda i, j: (0, j)),
                  pl.BlockSpec((d, tn), lambda i, j: (0, j))],
        out_specs=pl.BlockSpec((tm, tn), lambda i, j: (i, j)),
        out_shape=jax.ShapeDtypeStruct((m, f), BF16),
        compiler_params=_params(("parallel", "arbitrary"), 52),
        name="ffn_gate_up",
    )(x, wg, wu)


def _down_res_kernel(a_ref, w_ref, r_ref, o_ref, *maybe_ob_ref):
    y = r_ref[...] + jnp.dot(a_ref[...], w_ref[...], preferred_element_type=F32)
    o_ref[...] = y
    for ob_ref in maybe_ob_ref:
        ob_ref[...] = y.astype(ob_ref.dtype)


def _down_res(a, w, res, tm, tn, emit_bf16):
    m, f = a.shape
    d = w.shape[1]
    out_spec = pl.BlockSpec((tm, tn), lambda i, j: (i, j))
    out_specs, out_shape = [out_spec], [jax.ShapeDtypeStruct((m, d), F32)]
    if emit_bf16:
        out_specs.append(out_spec)
        out_shape.append(jax.ShapeDtypeStruct((m, d), BF16))
    return pl.pallas_call(
        _down_res_kernel,
        grid=(m // tm, d // tn),
        in_specs=[pl.BlockSpec((tm, f), lambda i, j: (i, 0)),
                  pl.BlockSpec((f, tn), lambda i, j: (0, j)),
                  pl.BlockSpec((tm, tn), lambda i, j: (i, j))],
        out_specs=out_specs,
        out_shape=out_shape,
        compiler_params=_params(("parallel", "arbitrary"), 56),
        name="ffn_down_res",
    )(a, w, res)


def _swiglu_res(x, res, wg, wu, wd, tm_up, tm_down, normalize, emit_bf16):
    a = _gate_up(x, wg, wu, tm_up, 512, normalize)
    return _down_res(a, wd, res, tm_down, 256, emit_bf16)


def _in_proj_kernel(x_ref, w_ref, cs_ref, cos_ref, sa_ref, sb_ref, o_ref, rstd_ref, *, rope_lo, rope_hi):
    j = pl.program_id(1)
    tn = w_ref.shape[1]
    is_rope = jnp.logical_and(j >= rope_lo, j < rope_hi)
    pl.when(j == 0)(functools.partial(_store_row_rstd, x_ref, rstd_ref))

    def body(rope):
        x = x_ref[...]
        for sub in range(tn // PROJ_SUB):
            cols = slice(sub * PROJ_SUB, (sub + 1) * PROJ_SUB)
            y = jnp.dot(x, w_ref[:, cols], preferred_element_type=F32) * cs_ref[:, cols]
            for c in range(PROJ_SUB // HEAD_DIM):
                yc = y[:, c * HEAD_DIM:(c + 1) * HEAD_DIM] * rstd_ref[...]
                if rope:
                    upper = pltpu.roll(yc, HEAD_DIM - ROPE_DIM // 2, 1)
                    lower = pltpu.roll(yc, ROPE_DIM // 2, 1)
                    yc = yc * cos_ref[...] + upper * sa_ref[...] + lower * sb_ref[...]
                o_ref[sub * (PROJ_SUB // HEAD_DIM) + c] = yc.astype(o_ref.dtype)

    pl.when(is_rope)(lambda: body(True))
    pl.when(jnp.logical_not(is_rope))(lambda: body(False))


def _in_proj(x, w, col_scale, rope_tabs, tm, tn):
    m, d = x.shape
    n = w.shape[1]
    cpb = tn // HEAD_DIM
    rope_lo = (3 * NA_WIDTH) // tn
    rope_hi = (3 * NA_WIDTH + 2 * DIFF_WIDTH) // tn
    tab_spec = pl.BlockSpec((tm, HEAD_DIM), lambda i, j: (i, 0))
    return pl.pallas_call(
        functools.partial(_in_proj_kernel, rope_lo=rope_lo, rope_hi=rope_hi),
        grid=(m // tm, n // tn),
        in_specs=[pl.BlockSpec((tm, d), lambda i, j: (i, 0)),
                  pl.BlockSpec((d, tn), lambda i, j: (0, j)),
                  pl.BlockSpec((1, tn), lambda i, j: (0, j)),
                  tab_spec, tab_spec, tab_spec],
        out_specs=pl.BlockSpec((cpb, tm, HEAD_DIM), lambda i, j: (j, i, 0)),
        out_shape=jax.ShapeDtypeStruct((n // HEAD_DIM, m, HEAD_DIM), BF16),
        scratch_shapes=[pltpu.VMEM((tm, HEAD_DIM), F32)],
        compiler_params=_params(("parallel", "arbitrary"), 56),
        name="in_proj_rope",
    )(x, w, col_scale, *rope_tabs)


def _rope_tables(pos):
    half = ROPE_DIM // 2
    inv = ROPE_THETA ** (-2.0 * jnp.arange(half, dtype=F32) / ROPE_DIM)
    ang = pos.astype(F32)[:, None] * inv[None, :]
    cos, sin = jnp.cos(ang), jnp.sin(ang)
    n = pos.shape[0]
    ones = jnp.ones((n, HEAD_DIM - ROPE_DIM), F32)
    zeros_h = jnp.zeros((n, half), F32)
    zeros_r = jnp.zeros((n, HEAD_DIM - ROPE_DIM), F32)
    cos_t = jnp.concatenate([cos, cos, ones], axis=1)
    sa_t = jnp.concatenate([-sin, zeros_h, zeros_r], axis=1)
    sb_t = jnp.concatenate([zeros_h, sin, zeros_r], axis=1)
    return cos_t, sa_t, sb_t


def _na_bias_kernel(rpb_ref, o_ref):
    h = pl.program_id(0)
    shape = (GRID_W, 2 * GRID_W)
    c = lax.broadcasted_iota(jnp.int32, shape, 0)
    lane = lax.broadcasted_iota(jnp.int32, shape, 1)
    second = lane >= GRID_W
    cc = jnp.bitwise_and(lane, GRID_W - 1)
    rel = cc - c + (WIN_W - 1)
    c0 = jnp.clip(c - WIN_W // 2, 0, GRID_W - WIN_W)
    col_ok = jnp.logical_and(cc >= c0, cc < c0 + WIN_W)
    n_dr = 2 * WIN_H - 1
    n_dc = 2 * WIN_W - 1
    base = h * (n_dr * n_dc)

    pair = []
    for d in range(n_dr - 1):
        val = jnp.zeros(shape, F32)
        for k in range(n_dc):
            w0 = rpb_ref[base + d * n_dc + k]
            w1 = rpb_ref[base + (d + 1) * n_dc + k]
            val = jnp.where(rel == k, jnp.where(second, w1, w0), val)
        pair.append(jnp.where(col_ok, val, NEG))

    key_rows = 3 * NA_ROWS_Q
    row_of_lane = second.astype(jnp.int32)
    for qr in range(NA_ROWS_Q):
        for jp in range(key_rows // 2):
            j = 2 * jp + row_of_lane
            tile = pair[2 * jp + (NA_ROWS_Q - 1) - qr]
            rows = pl.ds(qr * GRID_W, GRID_W)
            cols = pl.ds(jp * 2 * GRID_W, 2 * GRID_W)
            first_ok = j >= NA_ROWS_Q
            mid_ok = jnp.logical_and(j >= qr, j <= qr + WIN_H - 1)
            last_ok = j <= WIN_H - 1
            o_ref[0, 0, rows, cols] = jnp.where(first_ok, tile, NEG)
            o_ref[1, 0, rows, cols] = jnp.where(mid_ok, tile, NEG)
            o_ref[2, 0, rows, cols] = jnp.where(last_ok, tile, NEG)


def _na_bias(rpb):
    return pl.pallas_call(
        _na_bias_kernel,
        grid=(NA_HEADS,),
        in_specs=[pl.BlockSpec(memory_space=pltpu.SMEM)],
        out_specs=pl.BlockSpec((3, 1, NA_Q, NA_KEYS), lambda h: (0, h, 0, 0)),
        out_shape=jax.ShapeDtypeStruct((3, NA_HEADS, NA_Q, NA_KEYS), F32),
        compiler_params=_params(("arbitrary",), 32),
        name="na_bias_table",
    )(rpb.reshape(-1))


def _na_kernel(q_ref, k0_ref, k1_ref, k2_ref, v0_ref, v1_ref, v2_ref, km_ref, vm_ref, b_ref, g_ref, o_ref):
    lane = lax.broadcasted_iota(jnp.int32, (1, META_PAD), 1)
    meta_mask = jnp.where(lane < N_META, 0.0, NEG).astype(F32)

    for hh in range(q_ref.shape[0]):
        q = q_ref[hh]
        s = [lax.dot_general(q, k_ref[hh], NT_DIMS, preferred_element_type=F32)
             + b_ref[0, hh, :, j * NA_Q:(j + 1) * NA_Q]
             for j, k_ref in enumerate((k0_ref, k1_ref, k2_ref))]
        s.append(lax.dot_general(q, km_ref[hh], NT_DIMS, preferred_element_type=F32) + meta_mask)

        mx = functools.reduce(jnp.maximum, [jnp.max(x, axis=1, keepdims=True) for x in s])
        p = [jnp.exp(x - mx) for x in s]
        denom = functools.reduce(jnp.add, [jnp.sum(x, axis=1, keepdims=True) for x in p])
        vals = (v0_ref[hh], v1_ref[hh], v2_ref[hh], vm_ref[hh])
        o = functools.reduce(jnp.add, [jnp.dot(x.astype(BF16), v, preferred_element_type=F32)
                                       for x, v in zip(p, vals)])
        o = o * (1.0 / denom)
        ms = jnp.mean(o * o, axis=-1, keepdims=True)
        o_ref[:, hh * HEAD_DIM:(hh + 1) * HEAD_DIM] = (o * lax.rsqrt(ms + EPS) * g_ref[hh]).astype(o_ref.dtype)


def _na_edge_type(g):
    local = jnp.where(g < NA_PROMPT_GROUPS, g, jnp.bitwise_and(g - NA_PROMPT_GROUPS, NA_SAMPLE_GROUPS - 1))
    last = jnp.where(g < NA_PROMPT_GROUPS, NA_PROMPT_GROUPS - 1, NA_SAMPLE_GROUPS - 1)
    return jnp.where(local == 0, 0, jnp.where(local == last, 2, 1))


def _na_attention(p3, pm3, bias, gain):
    hps = NA_HEADS_PER_STEP
    steps_per_group = NA_HEADS // hps

    def tok_spec(block0, shift):
        def index(hb, g):
            return (block0 + hb, jnp.clip(g + shift, 0, NA_GROUPS - 1), 0)
        return pl.BlockSpec((hps, NA_Q, HEAD_DIM), index)

    def meta_spec(block0):
        return pl.BlockSpec((hps, META_PAD, HEAD_DIM), lambda hb, g: (block0 + hb, 0, 0))

    k0, v0 = steps_per_group, 2 * steps_per_group
    return pl.pallas_call(
        _na_kernel,
        grid=(steps_per_group, NA_GROUPS),
        in_specs=[tok_spec(0, 0),
                  tok_spec(k0, -1), tok_spec(k0, 0), tok_spec(k0, 1),
                  tok_spec(v0, -1), tok_spec(v0, 0), tok_spec(v0, 1),
                  meta_spec(k0), meta_spec(v0),
                  pl.BlockSpec((1, hps, NA_Q, NA_KEYS), lambda hb, g: (_na_edge_type(g), hb, 0, 0)),
                  pl.BlockSpec((hps, 1, HEAD_DIM), lambda hb, g: (hb, 0, 0))],
        out_specs=pl.BlockSpec((NA_Q, hps * HEAD_DIM), lambda hb, g: (g, hb)),
        out_shape=jax.ShapeDtypeStruct((T_REAL, NA_WIDTH), BF16),
        compiler_params=_params(("parallel", "arbitrary"), 40),
        name="na_attention",
    )(p3, p3, p3, p3, p3, p3, p3, pm3, pm3, bias, gain.reshape(NA_HEADS, 1, HEAD_DIM))


def _diff_kernel(lq1_ref, lk1_ref, lq2_ref, lk2_ref, g_ref, q_ref, k_ref, v_ref,
                 km_ref, vm_ref, o_ref, s_s, sm_s, m_s, l_s, acc_s):
    tq = q_ref.shape[1]
    n_chunks = k_ref.shape[1] // DIFF_TK
    lane_tiles = DIFF_TK // HEAD_DIM
    lane = lax.broadcasted_iota(jnp.int32, (1, META_PAD), 1)
    meta_mask = jnp.where(lane < N_META, 0.0, NEG).astype(F32)

    for mp in range(2):
        for r0 in range(0, tq, DIFF_STREAM_ROWS):
            qrows = slice(r0, r0 + DIFF_STREAM_ROWS)
            q = q_ref[mp, qrows, :]
            s = lax.dot_general(q, km_ref[mp], NT_DIMS, preferred_element_type=F32) + meta_mask
            sm_s[mp, qrows, :] = s
            mrun = s
            for c in range(n_chunks):
                krows = slice(c * DIFF_TK, (c + 1) * DIFF_TK)
                s = lax.dot_general(q, k_ref[mp, krows, :], NT_DIMS, preferred_element_type=F32)
                s_s[mp, c, qrows, :] = s
                tiles = [s[:, t * HEAD_DIM:(t + 1) * HEAD_DIM] for t in range(lane_tiles)]
                mrun = jnp.maximum(mrun, functools.reduce(jnp.maximum, tiles))
            m_s[mp, qrows, :] = jnp.broadcast_to(jnp.max(mrun, axis=1, keepdims=True),
                                                 (DIFF_STREAM_ROWS, HEAD_DIM))

    pm = [jnp.exp2(sm_s[mp] - m_s[mp]) for mp in range(2)]
    vm = jnp.concatenate([vm_ref[0], vm_ref[1]], axis=1)
    for mp in range(2):
        l_s[mp] = pm[mp]
    acc_s[...] = jnp.dot(jnp.concatenate(pm, axis=0).astype(BF16), vm, preferred_element_type=F32)

    for c in range(n_chunks):
        krows = slice(c * DIFF_TK, (c + 1) * DIFF_TK)
        v = jnp.concatenate([v_ref[0, krows, :], v_ref[1, krows, :]], axis=1)
        ps = []
        for mp in range(2):
            m_b = m_s[mp]
            lsum = l_s[mp]
            parts = []
            for t in range(lane_tiles):
                p = jnp.exp2(s_s[mp, c, :, t * HEAD_DIM:(t + 1) * HEAD_DIM] - m_b)
                lsum = lsum + p
                parts.append(p.astype(BF16))
            l_s[mp] = lsum
            ps.append(jnp.concatenate(parts, axis=1))
        acc_s[...] += jnp.dot(jnp.concatenate(ps, axis=0), v, preferred_element_type=F32)

    lam = (jnp.exp(jnp.sum(lq1_ref[...] * lk1_ref[...], axis=1, keepdims=True))
           - jnp.exp(jnp.sum(lq2_ref[...] * lk2_ref[...], axis=1, keepdims=True)) + LAM_INIT)
    l1 = jnp.sum(l_s[0], axis=1, keepdims=True)
    l2 = jnp.sum(l_s[1], axis=1, keepdims=True)
    o = acc_s[:tq, :] * (1.0 / l1) - acc_s[tq:, :] * (lam / l2)
    ms = jnp.mean(o * o, axis=-1, keepdims=True)
    o_ref[...] = ((o * lax.rsqrt(ms + SUBLN_EPS) * g_ref[...]) * (1.0 - LAM_INIT)).astype(o_ref.dtype)


def _diff_attention(p3, pm3, lq1, lk1, lq2, lk2, gain, row0, seq_len, n_seq, tq):
    q0 = (3 * NA_WIDTH) // (2 * HEAD_DIM)
    k0 = q0 + DIFF_HEADS
    v0 = k0 + DIFF_HEADS
    qt_per_seq = seq_len // tq
    n_chunks = seq_len // DIFF_TK

    def kv_spec(chunk0):
        return pl.BlockSpec((2, seq_len, HEAD_DIM),
                            lambda h, qt: (chunk0 + h, row0 // seq_len + qt // qt_per_seq, 0))

    def meta_spec(chunk0):
        return pl.BlockSpec((2, META_PAD, HEAD_DIM), lambda h, qt: (chunk0 + h, 0, 0))

    vec_spec = pl.BlockSpec((1, HEAD_DIM), lambda h, qt: (0, 0))
    return pl.pallas_call(
        _diff_kernel,
        grid=(DIFF_HEADS, n_seq * qt_per_seq),
        in_specs=[vec_spec, vec_spec, vec_spec, vec_spec,
                  pl.BlockSpec((1, 2 * HEAD_DIM), lambda h, qt: (0, 0)),
                  pl.BlockSpec((2, tq, HEAD_DIM), lambda h, qt: (q0 + h, row0 // tq + qt, 0)),
                  kv_spec(k0), kv_spec(v0), meta_spec(k0), meta_spec(v0)],
        out_specs=pl.BlockSpec((tq, 2 * HEAD_DIM), lambda h, qt: (qt, h)),
        out_shape=jax.ShapeDtypeStruct((n_seq * seq_len, DIFF_WIDTH), BF16),
        scratch_shapes=[pltpu.VMEM((2, n_chunks, tq, DIFF_TK), F32),
                        pltpu.VMEM((2, tq, META_PAD), F32),
                        pltpu.VMEM((2, tq, HEAD_DIM), F32),
                        pltpu.VMEM((2, tq, HEAD_DIM), F32),
                        pltpu.VMEM((2 * tq, 2 * HEAD_DIM), F32)],
        compiler_params=_params(("parallel", "arbitrary"), 52),
        name="diff_attention",
    )(lq1.reshape(1, -1), lk1.reshape(1, -1), lq2.reshape(1, -1), lk2.reshape(1, -1),
      gain.reshape(1, -1), p3, p3, p3, pm3, pm3)


def _out_proj_kernel(na_ref, dp_ref, ds_ref, w_ref, r_ref, o_ref, ob_ref, *, prompt_tiles):
    def body(d_ref):
        y = jnp.dot(na_ref[...], w_ref[:NA_WIDTH, :], preferred_element_type=F32)
        y += jnp.dot(d_ref[...], w_ref[NA_WIDTH:, :], preferred_element_type=F32)
        y = r_ref[...] + y
        o_ref[...] = y
        ob_ref[...] = y.astype(ob_ref.dtype)

    i = pl.program_id(0)
    pl.when(i < prompt_tiles)(lambda: body(dp_ref))
    pl.when(i >= prompt_tiles)(lambda: body(ds_ref))


def _out_proj(na, d_prompt, d_sample, w, res, tm, tn):
    m = na.shape[0]
    n = w.shape[1]
    pt = d_prompt.shape[0] // tm
    out_spec = pl.BlockSpec((tm, tn), lambda i, j: (i, j))
    return pl.pallas_call(
        functools.partial(_out_proj_kernel, prompt_tiles=pt),
        grid=(m // tm, n // tn),
        in_specs=[pl.BlockSpec((tm, NA_WIDTH), lambda i, j: (i, 0)),
                  pl.BlockSpec((tm, DIFF_WIDTH), lambda i, j: (jnp.minimum(i, pt - 1), 0)),
                  pl.BlockSpec((tm, DIFF_WIDTH), lambda i, j: (jnp.maximum(i - pt, 0), 0)),
                  pl.BlockSpec((NA_WIDTH + DIFF_WIDTH, tn), lambda i, j: (0, j)),
                  pl.BlockSpec((tm, tn), lambda i, j: (i, j))],
        out_specs=[out_spec, out_spec],
        out_shape=[jax.ShapeDtypeStruct((m, n), F32), jax.ShapeDtypeStruct((m, n), BF16)],
        compiler_params=_params(("parallel", "arbitrary"), 54),
        name="out_proj_res",
    )(na, d_prompt, d_sample, w, res)


def kernel(x_prompt, x_sample, meta_tokens, ffn1_norm, ffn1_wg, ffn1_wu, ffn1_wd, mix_norm, w_in, na_rpb, na_norm, diff_lq1, diff_lk1, diff_lq2, diff_lk2, diff_norm, w_out, ffn2_norm, ffn2_wg, ffn2_wu, ffn2_wd, final_norm):
    xp = x_prompt.reshape(PROMPT_LEN, D_MODEL)
    xs = x_sample.reshape(N_SAMPLES * SAMPLE_LEN, D_MODEL)
    meta = jnp.pad(meta_tokens, ((0, META_PAD - N_META), (0, 0)))

    wg1, wu1, wd1 = ffn1_wg[0].astype(BF16), ffn1_wu[0].astype(BF16), ffn1_wd[0].astype(BF16)
    gain2 = ffn2_norm[0].astype(F32)[:, None]
    wg2, wu2, wd2 = (gain2 * ffn2_wg[0]).astype(BF16), (gain2 * ffn2_wu[0]).astype(BF16), ffn2_wd[0].astype(BF16)
    w_in_b = (mix_norm[0].astype(F32)[:, None] * w_in[0]).astype(BF16)
    w_out_b = w_out[0].astype(BF16)

    col = jnp.arange(IN_WIDTH)
    is_q_na = col < NA_WIDTH
    is_q_diff = jnp.logical_and(col >= 3 * NA_WIDTH, col < 3 * NA_WIDTH + DIFF_WIDTH)
    col_scale = jnp.where(is_q_na, ATTN_SCALE, jnp.where(is_q_diff, ATTN_SCALE * LOG2E, 1.0))
    col_scale = col_scale.astype(F32).reshape(1, IN_WIDTH)

    pos_real = jnp.concatenate([jnp.arange(PROMPT_LEN, dtype=jnp.int32)]
                               + [jnp.arange(SAMPLE_LEN, dtype=jnp.int32)] * N_SAMPLES) + N_META
    pos_meta = jnp.arange(META_PAD, dtype=jnp.int32)

    xn1, x = _rmsnorm_concat(xp, xs, ffn1_norm[0], 256)
    h1, h1_b = _swiglu_res(xn1, x, wg1, wu1, wd1, 1024, 768, normalize=False, emit_bf16=True)
    xnm1 = _rmsnorm(meta, ffn1_norm[0], BF16, META_PAD)
    _, h1m_b = _swiglu_res(xnm1, meta, wg1, wu1, wd1, META_PAD, META_PAD, normalize=False, emit_bf16=True)

    p3 = _in_proj(h1_b, w_in_b, col_scale, _rope_tables(pos_real), 1024, 1024)
    pm3 = _in_proj(h1m_b, w_in_b, col_scale, _rope_tables(pos_meta), META_PAD, 1024)
    bias = _na_bias(na_rpb[0])
    na = _na_attention(p3, pm3, bias, na_norm[0])
    lam_args = (diff_lq1[0], diff_lk1[0], diff_lq2[0], diff_lk2[0], diff_norm[0])
    dd_prompt = _diff_attention(p3, pm3, *lam_args, row0=0, seq_len=PROMPT_LEN, n_seq=1, tq=256)
    dd_sample = _diff_attention(p3, pm3, *lam_args, row0=PROMPT_LEN, seq_len=SAMPLE_LEN, n_seq=N_SAMPLES, tq=512)
    h2, h2_b = _out_proj(na, dd_prompt, dd_sample, w_out_b, h1, 1024, 512)

    h3, = _swiglu_res(h2_b, h2, wg2, wu2, wd2, 1024, 768, normalize=True, emit_bf16=False)
    y_prompt, y_sample = _rmsnorm_split(h3, final_norm, PROMPT_LEN, 256)
    return (y_prompt.reshape(1, PROMPT_LEN, D_MODEL), y_sample.reshape(N_SAMPLES, SAMPLE_LEN, D_MODEL))
```

```python
import functools
import math

import jax
import jax.numpy as jnp
from jax import lax
from jax.experimental import pallas as pl
from jax.experimental.pallas import tpu as pltpu

F32 = jnp.float32
BF16 = jnp.bfloat16

D_MODEL = 4096
N_META = 16
META_PAD = 128
GRID_W = 64
WIN_H = 8
WIN_W = 16
NA_HEADS = 16
HEAD_DIM = 128
NA_WIDTH = NA_HEADS * HEAD_DIM
DIFF_HEADS = 8
DIFF_WIDTH = DIFF_HEADS * 2 * HEAD_DIM
IN_WIDTH = 3 * NA_WIDTH + 3 * DIFF_WIDTH
N_CHUNKS = IN_WIDTH // HEAD_DIM
D_FF = 11008
ROPE_THETA = 500000.0
ROPE_DIM = HEAD_DIM // 4
EPS = 1e-6
SUBLN_EPS = 1e-5
LAM_INIT = 0.8 - 0.6 * math.exp(-0.3 * 0)
ATTN_SCALE = HEAD_DIM ** -0.5
LOG2E = math.log2(math.e)
NEG = -1e30

PROMPT_LEN = 8192
SAMPLE_LEN = 4096
N_SAMPLES = 4
T_REAL = PROMPT_LEN + N_SAMPLES * SAMPLE_LEN

NA_Q = 256
NA_ROWS_Q = NA_Q // GRID_W
NA_KEYS = 3 * NA_Q
NA_GROUPS = T_REAL // NA_Q
NA_HEADS_PER_STEP = 8
NA_PROMPT_GROUPS = PROMPT_LEN // NA_Q
NA_SAMPLE_GROUPS = SAMPLE_LEN // NA_Q

DIFF_TK = 1024
DIFF_STREAM_ROWS = 128

MIB = 1024 * 1024
NT_DIMS = (((1,), (1,)), ((), ()))
PROJ_SUB = 256


def _params(semantics, vmem_mib):
    return pltpu.CompilerParams(dimension_semantics=semantics, vmem_limit_bytes=vmem_mib * MIB)


def _rmsnorm_kernel(x_ref, g_ref, o_ref, *, eps):
    x = x_ref[...]
    ms = jnp.mean(x * x, axis=-1, keepdims=True)
    o_ref[...] = (x * lax.rsqrt(ms + eps) * g_ref[...]).astype(o_ref.dtype)


def _rmsnorm(x, gain, out_dtype, tm):
    m, d = x.shape
    return pl.pallas_call(
        functools.partial(_rmsnorm_kernel, eps=EPS),
        grid=(m // tm,),
        in_specs=[pl.BlockSpec((tm, d), lambda i: (i, 0)),
                  pl.BlockSpec((1, d), lambda i: (0, 0))],
        out_specs=pl.BlockSpec((tm, d), lambda i: (i, 0)),
        out_shape=jax.ShapeDtypeStruct((m, d), out_dtype),
        compiler_params=_params(("parallel",), 32),
        name="rmsnorm",
    )(x, gain.reshape(1, d))


def _rmsnorm_concat_kernel(xp_ref, xs_ref, g_ref, o_ref, x_ref, *, eps, prompt_tiles):
    def emit(src_ref):
        x = src_ref[...]
        ms = jnp.mean(x * x, axis=-1, keepdims=True)
        o_ref[...] = (x * lax.rsqrt(ms + eps) * g_ref[...]).astype(o_ref.dtype)
        x_ref[...] = x

    i = pl.program_id(0)
    pl.when(i < prompt_tiles)(lambda: emit(xp_ref))
    pl.when(i >= prompt_tiles)(lambda: emit(xs_ref))


def _rmsnorm_concat(xp, xs, gain, tm):
    d = xp.shape[1]
    pt = xp.shape[0] // tm
    m = xp.shape[0] + xs.shape[0]
    row_spec = pl.BlockSpec((tm, d), lambda i: (i, 0))
    return pl.pallas_call(
        functools.partial(_rmsnorm_concat_kernel, eps=EPS, prompt_tiles=pt),
        grid=(m // tm,),
        in_specs=[pl.BlockSpec((tm, d), lambda i: (jnp.minimum(i, pt - 1), 0)),
                  pl.BlockSpec((tm, d), lambda i: (jnp.maximum(i - pt, 0), 0)),
                  pl.BlockSpec((1, d), lambda i: (0, 0))],
        out_specs=[row_spec, row_spec],
        out_shape=[jax.ShapeDtypeStruct((m, d), BF16), jax.ShapeDtypeStruct((m, d), F32)],
        compiler_params=_params(("arbitrary",), 40),
        name="rmsnorm_concat",
    )(xp, xs, gain.reshape(1, d))


def _rmsnorm_split_kernel(x_ref, g_ref, op_ref, os_ref, *, eps, prompt_tiles):
    x = x_ref[...]
    ms = jnp.mean(x * x, axis=-1, keepdims=True)
    y = x * lax.rsqrt(ms + eps) * g_ref[...]
    i = pl.program_id(0)

    @pl.when(i < prompt_tiles)
    def _():
        op_ref[...] = y

    @pl.when(i >= prompt_tiles)
    def _():
        os_ref[...] = y


def _rmsnorm_split(x, gain, n_prompt, tm):
    m, d = x.shape
    pt = n_prompt // tm
    return pl.pallas_call(
        functools.partial(_rmsnorm_split_kernel, eps=EPS, prompt_tiles=pt),
        grid=(m // tm,),
        in_specs=[pl.BlockSpec((tm, d), lambda i: (i, 0)),
                  pl.BlockSpec((1, d), lambda i: (0, 0))],
        out_specs=[pl.BlockSpec((tm, d), lambda i: (jnp.minimum(i, pt - 1), 0)),
                   pl.BlockSpec((tm, d), lambda i: (jnp.maximum(i - pt, 0), 0))],
        out_shape=[jax.ShapeDtypeStruct((n_prompt, d), F32), jax.ShapeDtypeStruct((m - n_prompt, d), F32)],
        compiler_params=_params(("arbitrary",), 40),
        name="rmsnorm_split",
    )(x, gain.reshape(1, d))


def _store_row_rstd(x_ref, rstd_ref):
    x = x_ref[...].astype(F32)
    ms = jnp.mean(x * x, axis=-1, keepdims=True)
    rstd_ref[...] = jnp.broadcast_to(lax.rsqrt(ms + EPS), rstd_ref.shape)


def _gate_up_kernel(x_ref, wg_ref, wu_ref, o_ref, *maybe_rstd_ref):
    for rstd_ref in maybe_rstd_ref:
        pl.when(pl.program_id(1) == 0)(functools.partial(_store_row_rstd, x_ref, rstd_ref))
    x = x_ref[...]
    g = jnp.dot(x, wg_ref[...], preferred_element_type=F32)
    u = jnp.dot(x, wu_ref[...], preferred_element_type=F32)
    for c in range(o_ref.shape[1] // HEAD_DIM):
        cols = slice(c * HEAD_DIM, (c + 1) * HEAD_DIM)
        gc, uc = g[:, cols], u[:, cols]
        for rstd_ref in maybe_rstd_ref:
            gc, uc = gc * rstd_ref[...], uc * rstd_ref[...]
        o_ref[:, cols] = (0.5 * jax.nn.silu(gc) * uc).astype(o_ref.dtype)


def _gate_up(x, wg, wu, tm, tn, normalize):
    m, d = x.shape
    f = wg.shape[1]
    return pl.pallas_call(
        _gate_up_kernel,
        scratch_shapes=[pltpu.VMEM((tm, HEAD_DIM), F32)] if normalize else [],
        grid=(m // tm, pl.cdiv(f, tn)),
        in_specs=[pl.BlockSpec((tm, d), lambda i, j: (i, 0)),
                  pl.BlockSpec((d, tn), lambda i, j: (0, j)),
                  pl.BlockSpec((d, tn), lambda i, j: (0, j))],
        out_specs=pl.BlockSpec((tm, tn), lambda i, j: (i, j)),
        out_shape=jax.ShapeDtypeStruct((m, f), BF16),
        compiler_params=_params(("parallel", "arbitrary"), 52),
        name="ffn_gate_up",
    )(x, wg, wu)


def _down_res_kernel(a_ref, w_ref, r_ref, o_ref, *maybe_ob_ref):
    y = r_ref[...] + jnp.dot(a_ref[...], w_ref[...], preferred_element_type=F32)
    o_ref[...] = y
    for ob_ref in maybe_ob_ref:
        ob_ref[...] = y.astype(ob_ref.dtype)


def _down_res(a, w, res, tm, tn, emit_bf16):
    m, f = a.shape
    d = w.shape[1]
    out_spec = pl.BlockSpec((tm, tn), lambda i, j: (i, j))
    out_specs, out_shape = [out_spec], [jax.ShapeDtypeStruct((m, d), F32)]
    if emit_bf16:
        out_specs.append(out_spec)
        out_shape.append(jax.ShapeDtypeStruct((m, d), BF16))
    return pl.pallas_call(
        _down_res_kernel,
        grid=(m // tm, d // tn),
        in_specs=[pl.BlockSpec((tm, f), lambda i, j: (i, 0)),
                  pl.BlockSpec((f, tn), lambda i, j: (0, j)),
                  pl.BlockSpec((tm, tn), lambda i, j: (i, j))],
        out_specs=out_specs,
        out_shape=out_shape,
        compiler_params=_params(("parallel", "arbitrary"), 56),
        name="ffn_down_res",
    )(a, w, res)


def _swiglu_res(x, res, wg, wu, wd, tm_up, down_tile, normalize, emit_bf16):
    a = _gate_up(x, wg, wu, tm_up, 512, normalize)
    return _down_res(a, wd, res, *down_tile, emit_bf16)


def _in_proj_kernel(x_ref, w_ref, cs_ref, cos_ref, sa_ref, sb_ref, o_ref, rstd_ref, *, rope_lo, rope_hi):
    j = pl.program_id(1)
    tn = w_ref.shape[1]
    is_rope = jnp.logical_and(j >= rope_lo, j < rope_hi)
    pl.when(j == 0)(functools.partial(_store_row_rstd, x_ref, rstd_ref))

    def body(rope):
        x = x_ref[...]
        for sub in range(tn // PROJ_SUB):
            cols = slice(sub * PROJ_SUB, (sub + 1) * PROJ_SUB)
            y = jnp.dot(x, w_ref[:, cols], preferred_element_type=F32) * cs_ref[:, cols]
            for c in range(PROJ_SUB // HEAD_DIM):
                yc = y[:, c * HEAD_DIM:(c + 1) * HEAD_DIM] * rstd_ref[...]
                if rope:
                    upper = pltpu.roll(yc, HEAD_DIM - ROPE_DIM // 2, 1)
                    lower = pltpu.roll(yc, ROPE_DIM // 2, 1)
                    yc = yc * cos_ref[...] + upper * sa_ref[...] + lower * sb_ref[...]
                o_ref[sub * (PROJ_SUB // HEAD_DIM) + c] = yc.astype(o_ref.dtype)

    pl.when(is_rope)(lambda: body(True))
    pl.when(jnp.logical_not(is_rope))(lambda: body(False))


def _in_proj(x, w, col_scale, rope_tabs, tm, tn):
    m, d = x.shape
    n = w.shape[1]
    cpb = tn // HEAD_DIM
    rope_lo = (3 * NA_WIDTH) // tn
    rope_hi = (3 * NA_WIDTH + 2 * DIFF_WIDTH) // tn
    tab_spec = pl.BlockSpec((tm, HEAD_DIM), lambda i, j: (i, 0))
    return pl.pallas_call(
        functools.partial(_in_proj_kernel, rope_lo=rope_lo, rope_hi=rope_hi),
        grid=(m // tm, n // tn),
        in_specs=[pl.BlockSpec((tm, d), lambda i, j: (i, 0)),
                  pl.BlockSpec((d, tn), lambda i, j: (0, j)),
                  pl.BlockSpec((1, tn), lambda i, j: (0, j)),
                  tab_spec, tab_spec, tab_spec],
        out_specs=pl.BlockSpec((cpb, tm, HEAD_DIM), lambda i, j: (j, i, 0)),
        out_shape=jax.ShapeDtypeStruct((n // HEAD_DIM, m, HEAD_DIM), BF16),
        scratch_shapes=[pltpu.VMEM((tm, HEAD_DIM), F32)],
        compiler_params=_params(("parallel", "arbitrary"), 56),
        name="in_proj_rope",
    )(x, w, col_scale, *rope_tabs)


def _rope_tables(pos):
    half = ROPE_DIM // 2
    inv = ROPE_THETA ** (-2.0 * jnp.arange(half, dtype=F32) / ROPE_DIM)
    ang = pos.astype(F32)[:, None] * inv[None, :]
    cos, sin = jnp.cos(ang), jnp.sin(ang)
    n = pos.shape[0]
    ones = jnp.ones((n, HEAD_DIM - ROPE_DIM), F32)
    zeros_h = jnp.zeros((n, half), F32)
    zeros_r = jnp.zeros((n, HEAD_DIM - ROPE_DIM), F32)
    cos_t = jnp.concatenate([cos, cos, ones], axis=1)
    sa_t = jnp.concatenate([-sin, zeros_h, zeros_r], axis=1)
    sb_t = jnp.concatenate([zeros_h, sin, zeros_r], axis=1)
    return cos_t, sa_t, sb_t


def _na_bias_kernel(rpb_ref, o_ref):
    h = pl.program_id(0)
    shape = (GRID_W, 2 * GRID_W)
    c = lax.broadcasted_iota(jnp.int32, shape, 0)
    lane = lax.broadcasted_iota(jnp.int32, shape, 1)
    second = lane >= GRID_W
    cc = jnp.bitwise_and(lane, GRID_W - 1)
    rel = cc - c + (WIN_W - 1)
    c0 = jnp.clip(c - WIN_W // 2, 0, GRID_W - WIN_W)
    col_ok = jnp.logical_and(cc >= c0, cc < c0 + WIN_W)
    n_dr = 2 * WIN_H - 1
    n_dc = 2 * WIN_W - 1
    base = h * (n_dr * n_dc)

    pair = []
    for d in range(n_dr - 1):
        val = jnp.zeros(shape, F32)
        for k in range(n_dc):
            w0 = rpb_ref[base + d * n_dc + k]
            w1 = rpb_ref[base + (d + 1) * n_dc + k]
            val = jnp.where(rel == k, jnp.where(second, w1, w0), val)
        pair.append(jnp.where(col_ok, val, NEG))

    key_rows = 3 * NA_ROWS_Q
    row_of_lane = second.astype(jnp.int32)
    for qr in range(NA_ROWS_Q):
        for jp in range(key_rows // 2):
            j = 2 * jp + row_of_lane
            tile = pair[2 * jp + (NA_ROWS_Q - 1) - qr]
            rows = pl.ds(qr * GRID_W, GRID_W)
            cols = pl.ds(jp * 2 * GRID_W, 2 * GRID_W)
            first_ok = j >= NA_ROWS_Q
            mid_ok = jnp.logical_and(j >= qr, j <= qr + WIN_H - 1)
            last_ok = j <= WIN_H - 1
            o_ref[0, 0, rows, cols] = jnp.where(first_ok, tile, NEG)
            o_ref[1, 0, rows, cols] = jnp.where(mid_ok, tile, NEG)
            o_ref[2, 0, rows, cols] = jnp.where(last_ok, tile, NEG)


def _na_bias(rpb):
    return pl.pallas_call(
        _na_bias_kernel,
        grid=(NA_HEADS,),
        in_specs=[pl.BlockSpec(memory_space=pltpu.SMEM)],
        out_specs=pl.BlockSpec((3, 1, NA_Q, NA_KEYS), lambda h: (0, h, 0, 0)),
        out_shape=jax.ShapeDtypeStruct((3, NA_HEADS, NA_Q, NA_KEYS), F32),
        compiler_params=_params(("arbitrary",), 32),
        name="na_bias_table",
    )(rpb.reshape(-1))


def _na_kernel(q_ref, k0_ref, k1_ref, k2_ref, v0_ref, v1_ref, v2_ref, km_ref, vm_ref, b_ref, g_ref, o_ref):
    lane = lax.broadcasted_iota(jnp.int32, (1, META_PAD), 1)
    meta_mask = jnp.where(lane < N_META, 0.0, NEG).astype(F32)

    for hh in range(q_ref.shape[0]):
        q = q_ref[hh]
        s = [lax.dot_general(q, k_ref[hh], NT_DIMS, preferred_element_type=F32)
             + b_ref[0, hh, :, j * NA_Q:(j + 1) * NA_Q]
             for j, k_ref in enumerate((k0_ref, k1_ref, k2_ref))]
        s.append(lax.dot_general(q, km_ref[hh], NT_DIMS, preferred_element_type=F32) + meta_mask)

        mx = functools.reduce(jnp.maximum, [jnp.max(x, axis=1, keepdims=True) for x in s])
        p = [jnp.exp(x - mx) for x in s]
        vals = [jnp.concatenate([v, jnp.ones(v.shape, v.dtype)], axis=1)
                for v in (v0_ref[hh], v1_ref[hh], v2_ref[hh], vm_ref[hh])]
        oe = functools.reduce(jnp.add, [jnp.dot(x.astype(BF16), v, preferred_element_type=F32)
                                        for x, v in zip(p, vals)])
        o = oe[:, :HEAD_DIM] * (1.0 / oe[:, HEAD_DIM:])
        ms = jnp.mean(o * o, axis=-1, keepdims=True)
        o_ref[:, hh * HEAD_DIM:(hh + 1) * HEAD_DIM] = (o * lax.rsqrt(ms + EPS) * g_ref[hh]).astype(o_ref.dtype)


def _na_edge_type(g):
    local = jnp.where(g < NA_PROMPT_GROUPS, g, jnp.bitwise_and(g - NA_PROMPT_GROUPS, NA_SAMPLE_GROUPS - 1))
    last = jnp.where(g < NA_PROMPT_GROUPS, NA_PROMPT_GROUPS - 1, NA_SAMPLE_GROUPS - 1)
    return jnp.where(local == 0, 0, jnp.where(local == last, 2, 1))


def _na_attention(p3, pm3, bias, gain):
    hps = NA_HEADS_PER_STEP
    steps_per_group = NA_HEADS // hps

    def tok_spec(block0, shift):
        def index(hb, g):
            return (block0 + hb, jnp.clip(g + shift, 0, NA_GROUPS - 1), 0)
        return pl.BlockSpec((hps, NA_Q, HEAD_DIM), index)

    def meta_spec(block0):
        return pl.BlockSpec((hps, META_PAD, HEAD_DIM), lambda hb, g: (block0 + hb, 0, 0))

    k0, v0 = steps_per_group, 2 * steps_per_group
    return pl.pallas_call(
        _na_kernel,
        grid=(steps_per_group, NA_GROUPS),
        in_specs=[tok_spec(0, 0),
                  tok_spec(k0, -1), tok_spec(k0, 0), tok_spec(k0, 1),
                  tok_spec(v0, -1), tok_spec(v0, 0), tok_spec(v0, 1),
                  meta_spec(k0), meta_spec(v0),
                  pl.BlockSpec((1, hps, NA_Q, NA_KEYS), lambda hb, g: (_na_edge_type(g), hb, 0, 0)),
                  pl.BlockSpec((hps, 1, HEAD_DIM), lambda hb, g: (hb, 0, 0))],
        out_specs=pl.BlockSpec((NA_Q, hps * HEAD_DIM), lambda hb, g: (g, hb)),
        out_shape=jax.ShapeDtypeStruct((T_REAL, NA_WIDTH), BF16),
        compiler_params=_params(("parallel", "arbitrary"), 40),
        name="na_attention",
    )(p3, p3, p3, p3, p3, p3, p3, pm3, pm3, bias, gain.reshape(NA_HEADS, 1, HEAD_DIM))


def _diff_kernel(lq1_ref, lk1_ref, lq2_ref, lk2_ref, g_ref, q_ref, k_ref, v_ref,
                 km_ref, vm_ref, o_ref, s_s, sm_s, m_s, l_s, acc_s):
    tq = q_ref.shape[1]
    n_chunks = k_ref.shape[1] // DIFF_TK
    lane_tiles = DIFF_TK // HEAD_DIM
    lane = lax.broadcasted_iota(jnp.int32, (1, META_PAD), 1)
    meta_mask = jnp.where(lane < N_META, 0.0, NEG).astype(F32)

    for mp in range(2):
        for r0 in range(0, tq, DIFF_STREAM_ROWS):
            qrows = slice(r0, r0 + DIFF_STREAM_ROWS)
            q = q_ref[mp, qrows, :]
            s = lax.dot_general(q, km_ref[mp], NT_DIMS, preferred_element_type=F32) + meta_mask
            sm_s[mp, qrows, :] = s
            mrun = s
            for c in range(n_chunks):
                krows = slice(c * DIFF_TK, (c + 1) * DIFF_TK)
                s = lax.dot_general(q, k_ref[mp, krows, :], NT_DIMS, preferred_element_type=F32)
                s_s[mp, c, qrows, :] = s
                tiles = [s[:, t * HEAD_DIM:(t + 1) * HEAD_DIM] for t in range(lane_tiles)]
                mrun = jnp.maximum(mrun, functools.reduce(jnp.maximum, tiles))
            m_s[mp, qrows, :] = jnp.broadcast_to(jnp.max(mrun, axis=1, keepdims=True),
                                                 (DIFF_STREAM_ROWS, HEAD_DIM))

    pm = [jnp.exp2(sm_s[mp] - m_s[mp]) for mp in range(2)]
    vm = jnp.concatenate([vm_ref[0], vm_ref[1]], axis=1)
    for mp in range(2):
        l_s[mp] = pm[mp]
    acc_s[...] = jnp.dot(jnp.concatenate(pm, axis=0).astype(BF16), vm, preferred_element_type=F32)

    for c in range(n_chunks):
        krows = slice(c * DIFF_TK, (c + 1) * DIFF_TK)
        v = jnp.concatenate([v_ref[0, krows, :], v_ref[1, krows, :]], axis=1)
        ps = []
        for mp in range(2):
            m_b = m_s[mp]
            lsum = l_s[mp]
            parts = []
            for t in range(lane_tiles):
                p = jnp.exp2(s_s[mp, c, :, t * HEAD_DIM:(t + 1) * HEAD_DIM] - m_b)
                lsum = lsum + p
                parts.append(p.astype(BF16))
            l_s[mp] = lsum
            ps.append(jnp.concatenate(parts, axis=1))
        acc_s[...] += jnp.dot(jnp.concatenate(ps, axis=0), v, preferred_element_type=F32)

    lam = (jnp.exp(jnp.sum(lq1_ref[...] * lk1_ref[...], axis=1, keepdims=True))
           - jnp.exp(jnp.sum(lq2_ref[...] * lk2_ref[...], axis=1, keepdims=True)) + LAM_INIT)
    l1 = jnp.sum(l_s[0], axis=1, keepdims=True)
    l2 = jnp.sum(l_s[1], axis=1, keepdims=True)
    o = acc_s[:tq, :] * (1.0 / l1) - acc_s[tq:, :] * (lam / l2)
    ms = jnp.mean(o * o, axis=-1, keepdims=True)
    o_ref[...] = ((o * lax.rsqrt(ms + SUBLN_EPS) * g_ref[...]) * (1.0 - LAM_INIT)).astype(o_ref.dtype)


def _diff_attention(p3, pm3, lq1, lk1, lq2, lk2, gain, row0, seq_len, n_seq, tq):
    q0 = (3 * NA_WIDTH) // (2 * HEAD_DIM)
    k0 = q0 + DIFF_HEADS
    v0 = k0 + DIFF_HEADS
    qt_per_seq = seq_len // tq
    n_chunks = seq_len // DIFF_TK

    def kv_spec(chunk0):
        return pl.BlockSpec((2, seq_len, HEAD_DIM),
                            lambda h, qt: (chunk0 + h, row0 // seq_len + qt // qt_per_seq, 0))

    def meta_spec(chunk0):
        return pl.BlockSpec((2, META_PAD, HEAD_DIM), lambda h, qt: (chunk0 + h, 0, 0))

    vec_spec = pl.BlockSpec((1, HEAD_DIM), lambda h, qt: (0, 0))
    return pl.pallas_call(
        _diff_kernel,
        grid=(DIFF_HEADS, n_seq * qt_per_seq),
        in_specs=[vec_spec, vec_spec, vec_spec, vec_spec,
                  pl.BlockSpec((1, 2 * HEAD_DIM), lambda h, qt: (0, 0)),
                  pl.BlockSpec((2, tq, HEAD_DIM), lambda h, qt: (q0 + h, row0 // tq + qt, 0)),
                  kv_spec(k0), kv_spec(v0), meta_spec(k0), meta_spec(v0)],
        out_specs=pl.BlockSpec((tq, 2 * HEAD_DIM), lambda h, qt: (qt, h)),
        out_shape=jax.ShapeDtypeStruct((n_seq * seq_len, DIFF_WIDTH), BF16),
        scratch_shapes=[pltpu.VMEM((2, n_chunks, tq, DIFF_TK), F32),
                        pltpu.VMEM((2, tq, META_PAD), F32),
                        pltpu.VMEM((2, tq, HEAD_DIM), F32),
                        pltpu.VMEM((2, tq, HEAD_DIM), F32),
                        pltpu.VMEM((2 * tq, 2 * HEAD_DIM), F32)],
        compiler_params=_params(("parallel", "arbitrary"), 52),
        name="diff_attention",
    )(lq1.reshape(1, -1), lk1.reshape(1, -1), lq2.reshape(1, -1), lk2.reshape(1, -1),
      gain.reshape(1, -1), p3, p3, p3, pm3, pm3)


def _out_proj_kernel(na_ref, dp_ref, ds_ref, w_ref, r_ref, o_ref, ob_ref, *, prompt_tiles):
    def body(d_ref):
        y = jnp.dot(na_ref[...], w_ref[:NA_WIDTH, :], preferred_element_type=F32)
        y += jnp.dot(d_ref[...], w_ref[NA_WIDTH:, :], preferred_element_type=F32)
        y = r_ref[...] + y
        o_ref[...] = y
        ob_ref[...] = y.astype(ob_ref.dtype)

    i = pl.program_id(0)
    pl.when(i < prompt_tiles)(lambda: body(dp_ref))
    pl.when(i >= prompt_tiles)(lambda: body(ds_ref))


def _out_proj(na, d_prompt, d_sample, w, res, tm, tn):
    m = na.shape[0]
    n = w.shape[1]
    pt = d_prompt.shape[0] // tm
    out_spec = pl.BlockSpec((tm, tn), lambda i, j: (i, j))
    return pl.pallas_call(
        functools.partial(_out_proj_kernel, prompt_tiles=pt),
        grid=(m // tm, n // tn),
        in_specs=[pl.BlockSpec((tm, NA_WIDTH), lambda i, j: (i, 0)),
                  pl.BlockSpec((tm, DIFF_WIDTH), lambda i, j: (jnp.minimum(i, pt - 1), 0)),
                  pl.BlockSpec((tm, DIFF_WIDTH), lambda i, j: (jnp.maximum(i - pt, 0), 0)),
                  pl.BlockSpec((NA_WIDTH + DIFF_WIDTH, tn), lambda i, j: (0, j)),
                  pl.BlockSpec((tm, tn), lambda i, j: (i, j))],
        out_specs=[out_spec, out_spec],
        out_shape=[jax.ShapeDtypeStruct((m, n), F32), jax.ShapeDtypeStruct((m, n), BF16)],
        compiler_params=_params(("parallel", "arbitrary"), 54),
        name="out_proj_res",
    )(na, d_prompt, d_sample, w, res)


def kernel(x_prompt, x_sample, meta_tokens, ffn1_norm, ffn1_wg, ffn1_wu, ffn1_wd, mix_norm, w_in, na_rpb, na_norm, diff_lq1, diff_lk1, diff_lq2, diff_lk2, diff_norm, w_out, ffn2_norm, ffn2_wg, ffn2_wu, ffn2_wd, final_norm):
    xp = x_prompt.reshape(PROMPT_LEN, D_MODEL)
    xs = x_sample.reshape(N_SAMPLES * SAMPLE_LEN, D_MODEL)
    meta = jnp.pad(meta_tokens, ((0, META_PAD - N_META), (0, 0)))

    wg1, wu1, wd1 = ffn1_wg[0].astype(BF16), ffn1_wu[0].astype(BF16), ffn1_wd[0].astype(BF16)
    gain2 = ffn2_norm[0].astype(F32)[:, None]
    wg2, wu2, wd2 = (gain2 * ffn2_wg[0]).astype(BF16), (gain2 * ffn2_wu[0]).astype(BF16), ffn2_wd[0].astype(BF16)
    w_in_b = (mix_norm[0].astype(F32)[:, None] * w_in[0]).astype(BF16)
    w_out_b = w_out[0].astype(BF16)

    col = jnp.arange(IN_WIDTH)
    is_q_na = col < NA_WIDTH
    is_q_diff = jnp.logical_and(col >= 3 * NA_WIDTH, col < 3 * NA_WIDTH + DIFF_WIDTH)
    col_scale = jnp.where(is_q_na, ATTN_SCALE, jnp.where(is_q_diff, ATTN_SCALE * LOG2E, 1.0))
    col_scale = col_scale.astype(F32).reshape(1, IN_WIDTH)

    pos_real = jnp.concatenate([jnp.arange(PROMPT_LEN, dtype=jnp.int32)]
                               + [jnp.arange(SAMPLE_LEN, dtype=jnp.int32)] * N_SAMPLES) + N_META
    pos_meta = jnp.arange(META_PAD, dtype=jnp.int32)

    xn1, x = _rmsnorm_concat(xp, xs, ffn1_norm[0], 256)
    h1, h1_b = _swiglu_res(xn1, x, wg1, wu1, wd1, 1024, (768, 256), normalize=False, emit_bf16=True)
    xnm1 = _rmsnorm(meta, ffn1_norm[0], BF16, META_PAD)
    _, h1m_b = _swiglu_res(xnm1, meta, wg1, wu1, wd1, META_PAD, (META_PAD, 256), normalize=False, emit_bf16=True)

    p3 = _in_proj(h1_b, w_in_b, col_scale, _rope_tables(pos_real), 1024, 1024)
    pm3 = _in_proj(h1m_b, w_in_b, col_scale, _rope_tables(pos_meta), META_PAD, 1024)
    bias = _na_bias(na_rpb[0])
    na = _na_attention(p3, pm3, bias, na_norm[0])
    lam_args = (diff_lq1[0], diff_lk1[0], diff_lq2[0], diff_lk2[0], diff_norm[0])
    dd_prompt = _diff_attention(p3, pm3, *lam_args, row0=0, seq_len=PROMPT_LEN, n_seq=1, tq=256)
    dd_sample = _diff_attention(p3, pm3, *lam_args, row0=PROMPT_LEN, seq_len=SAMPLE_LEN, n_seq=N_SAMPLES, tq=512)
    h2, h2_b = _out_proj(na, dd_prompt, dd_sample, w_out_b, h1, 1024, 512)

    h3, = _swiglu_res(h2_b, h2, wg2, wu2, wd2, 1024, (512, 512), normalize=True, emit_bf16=False)
    y_prompt, y_sample = _rmsnorm_split(h3, final_norm, PROMPT_LEN, 256)
    return (y_prompt.reshape(1, PROMPT_LEN, D_MODEL), y_sample.reshape(N_SAMPLES, SAMPLE_LEN, D_MODEL))
```

```python
import functools
import math

import jax
import jax.numpy as jnp
from jax import lax
from jax.experimental import pallas as pl
from jax.experimental.pallas import tpu as pltpu

F32 = jnp.float32
BF16 = jnp.bfloat16

D_MODEL = 4096
N_META = 16
META_PAD = 128
GRID_W = 64
WIN_H = 8
WIN_W = 16
NA_HEADS = 16
HEAD_DIM = 128
NA_WIDTH = NA_HEADS * HEAD_DIM
DIFF_HEADS = 8
DIFF_WIDTH = DIFF_HEADS * 2 * HEAD_DIM
IN_WIDTH = 3 * NA_WIDTH + 3 * DIFF_WIDTH
N_CHUNKS = IN_WIDTH // HEAD_DIM
D_FF = 11008
ROPE_THETA = 500000.0
ROPE_DIM = HEAD_DIM // 4
EPS = 1e-6
SUBLN_EPS = 1e-5
LAM_INIT = 0.8 - 0.6 * math.exp(-0.3 * 0)
ATTN_SCALE = HEAD_DIM ** -0.5
LOG2E = math.log2(math.e)
NEG = -1e30

PROMPT_LEN = 8192
SAMPLE_LEN = 4096
N_SAMPLES = 4
T_REAL = PROMPT_LEN + N_SAMPLES * SAMPLE_LEN

NA_Q = 256
NA_ROWS_Q = NA_Q // GRID_W
NA_KEYS = 3 * NA_Q
NA_GROUPS = T_REAL // NA_Q
NA_HEADS_PER_STEP = 8
NA_PROMPT_GROUPS = PROMPT_LEN // NA_Q
NA_SAMPLE_GROUPS = SAMPLE_LEN // NA_Q

DIFF_TK = 1024
DIFF_STREAM_ROWS = 128

MIB = 1024 * 1024
NT_DIMS = (((1,), (1,)), ((), ()))
PROJ_SUB = 256


def _params(semantics, vmem_mib):
    return pltpu.CompilerParams(dimension_semantics=semantics, vmem_limit_bytes=vmem_mib * MIB)


def _rmsnorm_kernel(x_ref, g_ref, o_ref, *, eps):
    x = x_ref[...]
    ms = jnp.mean(x * x, axis=-1, keepdims=True)
    o_ref[...] = (x * lax.rsqrt(ms + eps) * g_ref[...]).astype(o_ref.dtype)


def _rmsnorm(x, gain, out_dtype, tm):
    m, d = x.shape
    return pl.pallas_call(
        functools.partial(_rmsnorm_kernel, eps=EPS),
        grid=(m // tm,),
        in_specs=[pl.BlockSpec((tm, d), lambda i: (i, 0)),
                  pl.BlockSpec((1, d), lambda i: (0, 0))],
        out_specs=pl.BlockSpec((tm, d), lambda i: (i, 0)),
        out_shape=jax.ShapeDtypeStruct((m, d), out_dtype),
        compiler_params=_params(("parallel",), 32),
        name="rmsnorm",
    )(x, gain.reshape(1, d))


def _rmsnorm_concat_kernel(xp_ref, xs_ref, g_ref, o_ref, x_ref, *, eps, prompt_tiles):
    def emit(src_ref):
        x = src_ref[...]
        ms = jnp.mean(x * x, axis=-1, keepdims=True)
        o_ref[...] = (x * lax.rsqrt(ms + eps) * g_ref[...]).astype(o_ref.dtype)
        x_ref[...] = x

    i = pl.program_id(0)
    pl.when(i < prompt_tiles)(lambda: emit(xp_ref))
    pl.when(i >= prompt_tiles)(lambda: emit(xs_ref))


def _rmsnorm_concat(xp, xs, gain, tm):
    d = xp.shape[1]
    pt = xp.shape[0] // tm
    m = xp.shape[0] + xs.shape[0]
    row_spec = pl.BlockSpec((tm, d), lambda i: (i, 0))
    return pl.pallas_call(
        functools.partial(_rmsnorm_concat_kernel, eps=EPS, prompt_tiles=pt),
        grid=(m // tm,),
        in_specs=[pl.BlockSpec((tm, d), lambda i: (jnp.minimum(i, pt - 1), 0)),
                  pl.BlockSpec((tm, d), lambda i: (jnp.maximum(i - pt, 0), 0)),
                  pl.BlockSpec((1, d), lambda i: (0, 0))],
        out_specs=[row_spec, row_spec],
        out_shape=[jax.ShapeDtypeStruct((m, d), BF16), jax.ShapeDtypeStruct((m, d), F32)],
        compiler_params=_params(("arbitrary",), 40),
        name="rmsnorm_concat",
    )(xp, xs, gain.reshape(1, d))


def _rmsnorm_split_kernel(x_ref, g_ref, op_ref, os_ref, *, eps, prompt_tiles):
    x = x_ref[...]
    ms = jnp.mean(x * x, axis=-1, keepdims=True)
    y = x * lax.rsqrt(ms + eps) * g_ref[...]
    i = pl.program_id(0)

    @pl.when(i < prompt_tiles)
    def _():
        op_ref[...] = y

    @pl.when(i >= prompt_tiles)
    def _():
        os_ref[...] = y


def _rmsnorm_split(x, gain, n_prompt, tm):
    m, d = x.shape
    pt = n_prompt // tm
    return pl.pallas_call(
        functools.partial(_rmsnorm_split_kernel, eps=EPS, prompt_tiles=pt),
        grid=(m // tm,),
        in_specs=[pl.BlockSpec((tm, d), lambda i: (i, 0)),
                  pl.BlockSpec((1, d), lambda i: (0, 0))],
        out_specs=[pl.BlockSpec((tm, d), lambda i: (jnp.minimum(i, pt - 1), 0)),
                   pl.BlockSpec((tm, d), lambda i: (jnp.maximum(i - pt, 0), 0))],
        out_shape=[jax.ShapeDtypeStruct((n_prompt, d), F32), jax.ShapeDtypeStruct((m - n_prompt, d), F32)],
        compiler_params=_params(("arbitrary",), 40),
        name="rmsnorm_split",
    )(x, gain.reshape(1, d))


def _store_row_rstd(x_ref, rstd_ref):
    x = x_ref[...].astype(F32)
    ms = jnp.mean(x * x, axis=-1, keepdims=True)
    rstd_ref[...] = jnp.broadcast_to(lax.rsqrt(ms + EPS), rstd_ref.shape)


def _gate_up_kernel(x_ref, wg_ref, wu_ref, o_ref, *maybe_rstd_ref):
    for rstd_ref in maybe_rstd_ref:
        pl.when(pl.program_id(1) == 0)(functools.partial(_store_row_rstd, x_ref, rstd_ref))
    x = x_ref[...]
    g = jnp.dot(x, wg_ref[...], preferred_element_type=F32)
    u = jnp.dot(x, wu_ref[...], preferred_element_type=F32)
    for c in range(o_ref.shape[1] // HEAD_DIM):
        cols = slice(c * HEAD_DIM, (c + 1) * HEAD_DIM)
        gc, uc = g[:, cols], u[:, cols]
        for rstd_ref in maybe_rstd_ref:
            gc, uc = gc * rstd_ref[...], uc * rstd_ref[...]
        o_ref[:, cols] = (0.5 * jax.nn.silu(gc) * uc).astype(o_ref.dtype)


def _gate_up(x, wg, wu, tm, tn, normalize):
    m, d = x.shape
    f = wg.shape[1]
    return pl.pallas_call(
        _gate_up_kernel,
        scratch_shapes=[pltpu.VMEM((tm, HEAD_DIM), F32)] if normalize else [],
        grid=(m // tm, pl.cdiv(f, tn)),
        in_specs=[pl.BlockSpec((tm, d), lambda i, j: (i, 0)),
                  pl.BlockSpec((d, tn), lambda i, j: (0, j)),
                  pl.BlockSpec((d, tn), lambda i, j: (0, j))],
        out_specs=pl.BlockSpec((tm, tn), lambda i, j: (i, j)),
        out_shape=jax.ShapeDtypeStruct((m, f), BF16),
        compiler_params=_params(("parallel", "arbitrary"), 52),
        name="ffn_gate_up",
    )(x, wg, wu)


def _down_res_kernel(a_ref, w_ref, r_ref, o_ref, *maybe_ob_ref):
    y = r_ref[...] + jnp.dot(a_ref[...], w_ref[...], preferred_element_type=F32)
    o_ref[...] = y
    for ob_ref in maybe_ob_ref:
        ob_ref[...] = y.astype(ob_ref.dtype)


def _down_res(a, w, res, tm, tn, emit_bf16):
    m, f = a.shape
    d = w.shape[1]
    out_spec = pl.BlockSpec((tm, tn), lambda i, j: (i, j))
    out_specs, out_shape = [out_spec], [jax.ShapeDtypeStruct((m, d), F32)]
    if emit_bf16:
        out_specs.append(out_spec)
        out_shape.append(jax.ShapeDtypeStruct((m, d), BF16))
    return pl.pallas_call(
        _down_res_kernel,
        grid=(m // tm, d // tn),
        in_specs=[pl.BlockSpec((tm, f), lambda i, j: (i, 0)),
                  pl.BlockSpec((f, tn), lambda i, j: (0, j)),
                  pl.BlockSpec((tm, tn), lambda i, j: (i, j))],
        out_specs=out_specs,
        out_shape=out_shape,
        compiler_params=_params(("parallel", "arbitrary"), 56),
        name="ffn_down_res",
    )(a, w, res)


def _swiglu_res(x, res, wg, wu, wd, tm_up, down_tile, normalize, emit_bf16):
    a = _gate_up(x, wg, wu, tm_up, 512, normalize)
    return _down_res(a, wd, res, *down_tile, emit_bf16)


def _in_proj_kernel(x_ref, w_ref, cs_ref, cos_ref, sa_ref, sb_ref, o_ref, rstd_ref, *, rope_lo, rope_hi):
    j = pl.program_id(1)
    tn = w_ref.shape[1]
    is_rope = jnp.logical_and(j >= rope_lo, j < rope_hi)
    pl.when(j == 0)(functools.partial(_store_row_rstd, x_ref, rstd_ref))

    def body(rope):
        x = x_ref[...]
        for sub in range(tn // PROJ_SUB):
            cols = slice(sub * PROJ_SUB, (sub + 1) * PROJ_SUB)
            y = jnp.dot(x, w_ref[:, cols], preferred_element_type=F32) * cs_ref[:, cols]
            for c in range(PROJ_SUB // HEAD_DIM):
                yc = y[:, c * HEAD_DIM:(c + 1) * HEAD_DIM] * rstd_ref[...]
                if rope:
                    upper = pltpu.roll(yc, HEAD_DIM - ROPE_DIM // 2, 1)
                    lower = pltpu.roll(yc, ROPE_DIM // 2, 1)
                    yc = yc * cos_ref[...] + upper * sa_ref[...] + lower * sb_ref[...]
                o_ref[sub * (PROJ_SUB // HEAD_DIM) + c] = yc.astype(o_ref.dtype)

    pl.when(is_rope)(lambda: body(True))
    pl.when(jnp.logical_not(is_rope))(lambda: body(False))


def _in_proj(x, w, col_scale, rope_tabs, tm, tn):
    m, d = x.shape
    n = w.shape[1]
    cpb = tn // HEAD_DIM
    rope_lo = (3 * NA_WIDTH) // tn
    rope_hi = (3 * NA_WIDTH + 2 * DIFF_WIDTH) // tn
    tab_spec = pl.BlockSpec((tm, HEAD_DIM), lambda i, j: (i, 0))
    return pl.pallas_call(
        functools.partial(_in_proj_kernel, rope_lo=rope_lo, rope_hi=rope_hi),
        grid=(m // tm, n // tn),
        in_specs=[pl.BlockSpec((tm, d), lambda i, j: (i, 0)),
                  pl.BlockSpec((d, tn), lambda i, j: (0, j)),
                  pl.BlockSpec((1, tn), lambda i, j: (0, j)),
                  tab_spec, tab_spec, tab_spec],
        out_specs=pl.BlockSpec((cpb, tm, HEAD_DIM), lambda i, j: (j, i, 0)),
        out_shape=jax.ShapeDtypeStruct((n // HEAD_DIM, m, HEAD_DIM), BF16),
        scratch_shapes=[pltpu.VMEM((tm, HEAD_DIM), F32)],
        compiler_params=_params(("parallel", "arbitrary"), 56),
        name="in_proj_rope",
    )(x, w, col_scale, *rope_tabs)


def _rope_tables(pos):
    half = ROPE_DIM // 2
    inv = ROPE_THETA ** (-2.0 * jnp.arange(half, dtype=F32) / ROPE_DIM)
    ang = pos.astype(F32)[:, None] * inv[None, :]
    cos, sin = jnp.cos(ang), jnp.sin(ang)
    n = pos.shape[0]
    ones = jnp.ones((n, HEAD_DIM - ROPE_DIM), F32)
    zeros_h = jnp.zeros((n, half), F32)
    zeros_r = jnp.zeros((n, HEAD_DIM - ROPE_DIM), F32)
    cos_t = jnp.concatenate([cos, cos, ones], axis=1)
    sa_t = jnp.concatenate([-sin, zeros_h, zeros_r], axis=1)
    sb_t = jnp.concatenate([zeros_h, sin, zeros_r], axis=1)
    return cos_t, sa_t, sb_t


def _na_bias_kernel(rpb_ref, o_ref):
    h = pl.program_id(0)
    shape = (GRID_W, 2 * GRID_W)
    c = lax.broadcasted_iota(jnp.int32, shape, 0)
    lane = lax.broadcasted_iota(jnp.int32, shape, 1)
    second = lane >= GRID_W
    cc = jnp.bitwise_and(lane, GRID_W - 1)
    rel = cc - c + (WIN_W - 1)
    c0 = jnp.clip(c - WIN_W // 2, 0, GRID_W - WIN_W)
    col_ok = jnp.logical_and(cc >= c0, cc < c0 + WIN_W)
    n_dr = 2 * WIN_H - 1
    n_dc = 2 * WIN_W - 1
    base = h * (n_dr * n_dc)

    pair = []
    for d in range(n_dr - 1):
        val = jnp.zeros(shape, F32)
        for k in range(n_dc):
            w0 = rpb_ref[base + d * n_dc + k]
            w1 = rpb_ref[base + (d + 1) * n_dc + k]
            val = jnp.where(rel == k, jnp.where(second, w1, w0), val)
        pair.append(jnp.where(col_ok, val, NEG))

    key_rows = 3 * NA_ROWS_Q
    row_of_lane = second.astype(jnp.int32)
    for qr in range(NA_ROWS_Q):
        for jp in range(key_rows // 2):
            j = 2 * jp + row_of_lane
            tile = pair[2 * jp + (NA_ROWS_Q - 1) - qr]
            rows = pl.ds(qr * GRID_W, GRID_W)
            cols = pl.ds(jp * 2 * GRID_W, 2 * GRID_W)
            first_ok = j >= NA_ROWS_Q
            mid_ok = jnp.logical_and(j >= qr, j <= qr + WIN_H - 1)
            last_ok = j <= WIN_H - 1
            o_ref[0, 0, rows, cols] = jnp.where(first_ok, tile, NEG)
            o_ref[1, 0, rows, cols] = jnp.where(mid_ok, tile, NEG)
            o_ref[2, 0, rows, cols] = jnp.where(last_ok, tile, NEG)


def _na_bias(rpb):
    return pl.pallas_call(
        _na_bias_kernel,
        grid=(NA_HEADS,),
        in_specs=[pl.BlockSpec(memory_space=pltpu.SMEM)],
        out_specs=pl.BlockSpec((3, 1, NA_Q, NA_KEYS), lambda h: (0, h, 0, 0)),
        out_shape=jax.ShapeDtypeStruct((3, NA_HEADS, NA_Q, NA_KEYS), F32),
        compiler_params=_params(("arbitrary",), 32),
        name="na_bias_table",
    )(rpb.reshape(-1))


def _na_kernel(q_ref, k0_ref, k1_ref, k2_ref, v0_ref, v1_ref, v2_ref, km_ref, vm_ref, b_ref, g_ref, o_ref):
    lane = lax.broadcasted_iota(jnp.int32, (1, META_PAD), 1)
    meta_mask = jnp.where(lane < N_META, 0.0, NEG).astype(F32)

    for hh in range(q_ref.shape[0]):
        q = q_ref[hh]
        s = [lax.dot_general(q, k_ref[hh], NT_DIMS, preferred_element_type=F32)
             + b_ref[0, hh, :, j * NA_Q:(j + 1) * NA_Q]
             for j, k_ref in enumerate((k0_ref, k1_ref, k2_ref))]
        s.append(lax.dot_general(q, km_ref[hh], NT_DIMS, preferred_element_type=F32) + meta_mask)

        mx = functools.reduce(jnp.maximum, [jnp.max(x, axis=1, keepdims=True) for x in s])
        p = [jnp.exp(x - mx) for x in s]
        vals = [jnp.concatenate([v, jnp.ones(v.shape, v.dtype)], axis=1)
                for v in (v0_ref[hh], v1_ref[hh], v2_ref[hh], vm_ref[hh])]
        oe = functools.reduce(jnp.add, [jnp.dot(x.astype(BF16), v, preferred_element_type=F32)
                                        for x, v in zip(p, vals)])
        o = oe[:, :HEAD_DIM] * (1.0 / oe[:, HEAD_DIM:])
        ms = jnp.mean(o * o, axis=-1, keepdims=True)
        o_ref[:, hh * HEAD_DIM:(hh + 1) * HEAD_DIM] = (o * lax.rsqrt(ms + EPS) * g_ref[hh]).astype(o_ref.dtype)


def _na_edge_type(g):
    local = jnp.where(g < NA_PROMPT_GROUPS, g, jnp.bitwise_and(g - NA_PROMPT_GROUPS, NA_SAMPLE_GROUPS - 1))
    last = jnp.where(g < NA_PROMPT_GROUPS, NA_PROMPT_GROUPS - 1, NA_SAMPLE_GROUPS - 1)
    return jnp.where(local == 0, 0, jnp.where(local == last, 2, 1))


def _na_attention(p3, pm3, bias, gain):
    hps = NA_HEADS_PER_STEP
    steps_per_group = NA_HEADS // hps

    def tok_spec(block0, shift):
        def index(hb, g):
            return (block0 + hb, jnp.clip(g + shift, 0, NA_GROUPS - 1), 0)
        return pl.BlockSpec((hps, NA_Q, HEAD_DIM), index)

    def meta_spec(block0):
        return pl.BlockSpec((hps, META_PAD, HEAD_DIM), lambda hb, g: (block0 + hb, 0, 0))

    k0, v0 = steps_per_group, 2 * steps_per_group
    return pl.pallas_call(
        _na_kernel,
        grid=(steps_per_group, NA_GROUPS),
        in_specs=[tok_spec(0, 0),
                  tok_spec(k0, -1), tok_spec(k0, 0), tok_spec(k0, 1),
                  tok_spec(v0, -1), tok_spec(v0, 0), tok_spec(v0, 1),
                  meta_spec(k0), meta_spec(v0),
                  pl.BlockSpec((1, hps, NA_Q, NA_KEYS), lambda hb, g: (_na_edge_type(g), hb, 0, 0)),
                  pl.BlockSpec((hps, 1, HEAD_DIM), lambda hb, g: (hb, 0, 0))],
        out_specs=pl.BlockSpec((NA_Q, hps * HEAD_DIM), lambda hb, g: (g, hb)),
        out_shape=jax.ShapeDtypeStruct((T_REAL, NA_WIDTH), BF16),
        compiler_params=_params(("parallel", "arbitrary"), 40),
        name="na_attention",
    )(p3, p3, p3, p3, p3, p3, p3, pm3, pm3, bias, gain.reshape(NA_HEADS, 1, HEAD_DIM))


def _diff_kernel(lq1_ref, lk1_ref, lq2_ref, lk2_ref, g_ref, q_ref, k_ref, v_ref,
                 km_ref, vm_ref, o_ref, s_s, sm_s, m_s, l_s, acc_s):
    tq = q_ref.shape[1]
    n_chunks = k_ref.shape[1] // DIFF_TK
    lane_tiles = DIFF_TK // HEAD_DIM
    lane = lax.broadcasted_iota(jnp.int32, (1, META_PAD), 1)
    meta_mask = jnp.where(lane < N_META, 0.0, NEG).astype(F32)

    for mp in range(2):
        for r0 in range(0, tq, DIFF_STREAM_ROWS):
            qrows = slice(r0, r0 + DIFF_STREAM_ROWS)
            q = q_ref[mp, qrows, :]
            s = lax.dot_general(q, km_ref[mp], NT_DIMS, preferred_element_type=F32) + meta_mask
            sm_s[mp, qrows, :] = s
            mrun = s
            for c in range(n_chunks):
                krows = slice(c * DIFF_TK, (c + 1) * DIFF_TK)
                s = lax.dot_general(q, k_ref[mp, krows, :], NT_DIMS, preferred_element_type=F32)
                s_s[mp, c, qrows, :] = s
                tiles = [s[:, t * HEAD_DIM:(t + 1) * HEAD_DIM] for t in range(lane_tiles)]
                mrun = jnp.maximum(mrun, functools.reduce(jnp.maximum, tiles))
            m_s[mp, qrows, :] = jnp.broadcast_to(jnp.max(mrun, axis=1, keepdims=True),
                                                 (DIFF_STREAM_ROWS, HEAD_DIM))

    pm = [jnp.exp2(sm_s[mp] - m_s[mp]) for mp in range(2)]
    vm = jnp.concatenate([vm_ref[0], vm_ref[1]], axis=1)
    for mp in range(2):
        l_s[mp] = pm[mp]
    acc_s[...] = jnp.dot(jnp.concatenate(pm, axis=0).astype(BF16), vm, preferred_element_type=F32)

    for c in range(n_chunks):
        krows = slice(c * DIFF_TK, (c + 1) * DIFF_TK)
        v = jnp.concatenate([v_ref[0, krows, :], v_ref[1, krows, :]], axis=1)
        ps = []
        for mp in range(2):
            m_b = m_s[mp]
            lsum = l_s[mp]
            parts = []
            for t in range(lane_tiles):
                p = jnp.exp2(s_s[mp, c, :, t * HEAD_DIM:(t + 1) * HEAD_DIM] - m_b)
                lsum = lsum + p
                parts.append(p.astype(BF16))
            l_s[mp] = lsum
            ps.append(jnp.concatenate(parts, axis=1))
        acc_s[...] += jnp.dot(jnp.concatenate(ps, axis=0), v, preferred_element_type=F32)

    lam = (jnp.exp(jnp.sum(lq1_ref[...] * lk1_ref[...], axis=1, keepdims=True))
           - jnp.exp(jnp.sum(lq2_ref[...] * lk2_ref[...], axis=1, keepdims=True)) + LAM_INIT)
    l1 = jnp.sum(l_s[0], axis=1, keepdims=True)
    l2 = jnp.sum(l_s[1], axis=1, keepdims=True)
    o = acc_s[:tq, :] * (1.0 / l1) - acc_s[tq:, :] * (lam / l2)
    ms = jnp.mean(o * o, axis=-1, keepdims=True)
    o_ref[...] = ((o * lax.rsqrt(ms + SUBLN_EPS) * g_ref[...]) * (1.0 - LAM_INIT)).astype(o_ref.dtype)


def _diff_attention(p3, pm3, lq1, lk1, lq2, lk2, gain, row0, seq_len, n_seq, tq):
    q0 = (3 * NA_WIDTH) // (2 * HEAD_DIM)
    k0 = q0 + DIFF_HEADS
    v0 = k0 + DIFF_HEADS
    qt_per_seq = seq_len // tq
    n_chunks = seq_len // DIFF_TK

    def kv_spec(chunk0):
        return pl.BlockSpec((2, seq_len, HEAD_DIM),
                            lambda h, qt: (chunk0 + h, row0 // seq_len + qt // qt_per_seq, 0))

    def meta_spec(chunk0):
        return pl.BlockSpec((2, META_PAD, HEAD_DIM), lambda h, qt: (chunk0 + h, 0, 0))

    vec_spec = pl.BlockSpec((1, HEAD_DIM), lambda h, qt: (0, 0))
    return pl.pallas_call(
        _diff_kernel,
        grid=(DIFF_HEADS, n_seq * qt_per_seq),
        in_specs=[vec_spec, vec_spec, vec_spec, vec_spec,
                  pl.BlockSpec((1, 2 * HEAD_DIM), lambda h, qt: (0, 0)),
                  pl.BlockSpec((2, tq, HEAD_DIM), lambda h, qt: (q0 + h, row0 // tq + qt, 0)),
                  kv_spec(k0), kv_spec(v0), meta_spec(k0), meta_spec(v0)],
        out_specs=pl.BlockSpec((tq, 2 * HEAD_DIM), lambda h, qt: (qt, h)),
        out_shape=jax.ShapeDtypeStruct((n_seq * seq_len, DIFF_WIDTH), BF16),
        scratch_shapes=[pltpu.VMEM((2, n_chunks, tq, DIFF_TK), F32),
                        pltpu.VMEM((2, tq, META_PAD), F32),
                        pltpu.VMEM((2, tq, HEAD_DIM), F32),
                        pltpu.VMEM((2, tq, HEAD_DIM), F32),
                        pltpu.VMEM((2 * tq, 2 * HEAD_DIM), F32)],
        compiler_params=_params(("parallel", "arbitrary"), 52),
        name="diff_attention",
    )(lq1.reshape(1, -1), lk1.reshape(1, -1), lq2.reshape(1, -1), lk2.reshape(1, -1),
      gain.reshape(1, -1), p3, p3, p3, pm3, pm3)


def _out_proj_kernel(na_ref, dp_ref, ds_ref, w_ref, r_ref, o_ref, ob_ref, *, prompt_tiles):
    def body(d_ref):
        y = jnp.dot(na_ref[...], w_ref[:NA_WIDTH, :], preferred_element_type=F32)
        y += jnp.dot(d_ref[...], w_ref[NA_WIDTH:, :], preferred_element_type=F32)
        y = r_ref[...] + y
        o_ref[...] = y
        ob_ref[...] = y.astype(ob_ref.dtype)

    i = pl.program_id(0)
    pl.when(i < prompt_tiles)(lambda: body(dp_ref))
    pl.when(i >= prompt_tiles)(lambda: body(ds_ref))


def _out_proj(na, d_prompt, d_sample, w, res, tm, tn):
    m = na.shape[0]
    n = w.shape[1]
    pt = d_prompt.shape[0] // tm
    out_spec = pl.BlockSpec((tm, tn), lambda i, j: (i, j))
    return pl.pallas_call(
        functools.partial(_out_proj_kernel, prompt_tiles=pt),
        grid=(m // tm, n // tn),
        in_specs=[pl.BlockSpec((tm, NA_WIDTH), lambda i, j: (i, 0)),
                  pl.BlockSpec((tm, DIFF_WIDTH), lambda i, j: (jnp.minimum(i, pt - 1), 0)),
                  pl.BlockSpec((tm, DIFF_WIDTH), lambda i, j: (jnp.maximum(i - pt, 0), 0)),
                  pl.BlockSpec((NA_WIDTH + DIFF_WIDTH, tn), lambda i, j: (0, j)),
                  pl.BlockSpec((tm, tn), lambda i, j: (i, j))],
        out_specs=[out_spec, out_spec],
        out_shape=[jax.ShapeDtypeStruct((m, n), F32), jax.ShapeDtypeStruct((m, n), BF16)],
        compiler_params=_params(("parallel", "arbitrary"), 54),
        name="out_proj_res",
    )(na, d_prompt, d_sample, w, res)


def kernel(x_prompt, x_sample, meta_tokens, ffn1_norm, ffn1_wg, ffn1_wu, ffn1_wd, mix_norm, w_in, na_rpb, na_norm, diff_lq1, diff_lk1, diff_lq2, diff_lk2, diff_norm, w_out, ffn2_norm, ffn2_wg, ffn2_wu, ffn2_wd, final_norm):
    xp = x_prompt.reshape(PROMPT_LEN, D_MODEL)
    xs = x_sample.reshape(N_SAMPLES * SAMPLE_LEN, D_MODEL)
    meta = jnp.pad(meta_tokens, ((0, META_PAD - N_META), (0, 0)))

    wg1, wu1, wd1 = ffn1_wg[0].astype(BF16), ffn1_wu[0].astype(BF16), ffn1_wd[0].astype(BF16)
    gain2 = ffn2_norm[0].astype(F32)[:, None]
    wg2, wu2, wd2 = (gain2 * ffn2_wg[0]).astype(BF16), (gain2 * ffn2_wu[0]).astype(BF16), ffn2_wd[0].astype(BF16)
    w_in_b = (mix_norm[0].astype(F32)[:, None] * w_in[0]).astype(BF16)
    w_out_b = w_out[0].astype(BF16)

    col = jnp.arange(IN_WIDTH)
    is_q_na = col < NA_WIDTH
    is_q_diff = jnp.logical_and(col >= 3 * NA_WIDTH, col < 3 * NA_WIDTH + DIFF_WIDTH)
    col_scale = jnp.where(is_q_na, ATTN_SCALE, jnp.where(is_q_diff, ATTN_SCALE * LOG2E, 1.0))
    col_scale = col_scale.astype(F32).reshape(1, IN_WIDTH)

    pos_real = jnp.concatenate([jnp.arange(PROMPT_LEN, dtype=jnp.int32)]
                               + [jnp.arange(SAMPLE_LEN, dtype=jnp.int32)] * N_SAMPLES) + N_META
    pos_meta = jnp.arange(META_PAD, dtype=jnp.int32)

    xn1, x = _rmsnorm_concat(xp, xs, ffn1_norm[0], 256)
    h1, h1_b = _swiglu_res(xn1, x, wg1, wu1, wd1, 1024, (512, 512), normalize=False, emit_bf16=True)
    xnm1 = _rmsnorm(meta, ffn1_norm[0], BF16, META_PAD)
    _, h1m_b = _swiglu_res(xnm1, meta, wg1, wu1, wd1, META_PAD, (META_PAD, 256), normalize=False, emit_bf16=True)

    p3 = _in_proj(h1_b, w_in_b, col_scale, _rope_tables(pos_real), 1024, 1024)
    pm3 = _in_proj(h1m_b, w_in_b, col_scale, _rope_tables(pos_meta), META_PAD, 1024)
    bias = _na_bias(na_rpb[0])
    na = _na_attention(p3, pm3, bias, na_norm[0])
    lam_args = (diff_lq1[0], diff_lk1[0], diff_lq2[0], diff_lk2[0], diff_norm[0])
    dd_prompt = _diff_attention(p3, pm3, *lam_args, row0=0, seq_len=PROMPT_LEN, n_seq=1, tq=256)
    dd_sample = _diff_attention(p3, pm3, *lam_args, row0=PROMPT_LEN, seq_len=SAMPLE_LEN, n_seq=N_SAMPLES, tq=512)
    h2, h2_b = _out_proj(na, dd_prompt, dd_sample, w_out_b, h1, 1024, 512)

    h3, = _swiglu_res(h2_b, h2, wg2, wu2, wd2, 1024, (512, 512), normalize=True, emit_bf16=False)
    y_prompt, y_sample = _rmsnorm_split(h3, final_norm, PROMPT_LEN, 256)
    return (y_prompt.reshape(1, PROMPT_LEN, D_MODEL), y_sample.reshape(N_SAMPLES, SAMPLE_LEN, D_MODEL))
```

```python
import functools
import math

import jax
import jax.numpy as jnp
from jax import lax
from jax.experimental import pallas as pl
from jax.experimental.pallas import tpu as pltpu

F32 = jnp.float32
BF16 = jnp.bfloat16

D_MODEL = 4096
N_META = 16
META_PAD = 128
GRID_W = 64
WIN_H = 8
WIN_W = 16
NA_HEADS = 16
HEAD_DIM = 128
NA_WIDTH = NA_HEADS * HEAD_DIM
DIFF_HEADS = 8
DIFF_WIDTH = DIFF_HEADS * 2 * HEAD_DIM
IN_WIDTH = 3 * NA_WIDTH + 3 * DIFF_WIDTH
N_CHUNKS = IN_WIDTH // HEAD_DIM
D_FF = 11008
ROPE_THETA = 500000.0
ROPE_DIM = HEAD_DIM // 4
EPS = 1e-6
SUBLN_EPS = 1e-5
LAM_INIT = 0.8 - 0.6 * math.exp(-0.3 * 0)
ATTN_SCALE = HEAD_DIM ** -0.5
LOG2E = math.log2(math.e)
NEG = -1e30

PROMPT_LEN = 8192
SAMPLE_LEN = 4096
N_SAMPLES = 4
T_REAL = PROMPT_LEN + N_SAMPLES * SAMPLE_LEN

NA_Q = 256
NA_ROWS_Q = NA_Q // GRID_W
NA_KEYS = 3 * NA_Q
NA_GROUPS = T_REAL // NA_Q
NA_HEADS_PER_STEP = 8
NA_PROMPT_GROUPS = PROMPT_LEN // NA_Q
NA_SAMPLE_GROUPS = SAMPLE_LEN // NA_Q

DIFF_TK = 1024
DIFF_STREAM_ROWS = 128

MIB = 1024 * 1024
NT_DIMS = (((1,), (1,)), ((), ()))
PROJ_SUB = 256
GATE_UP_TN = 512


def _params(semantics, vmem_mib):
    return pltpu.CompilerParams(dimension_semantics=semantics, vmem_limit_bytes=vmem_mib * MIB)


def _rmsnorm_kernel(x_ref, g_ref, o_ref, *, eps):
    x = x_ref[...]
    ms = jnp.mean(x * x, axis=-1, keepdims=True)
    o_ref[...] = (x * lax.rsqrt(ms + eps) * g_ref[...]).astype(o_ref.dtype)


def _rmsnorm(x, gain, out_dtype, tm):
    m, d = x.shape
    return pl.pallas_call(
        functools.partial(_rmsnorm_kernel, eps=EPS),
        grid=(m // tm,),
        in_specs=[pl.BlockSpec((tm, d), lambda i: (i, 0)),
                  pl.BlockSpec((1, d), lambda i: (0, 0))],
        out_specs=pl.BlockSpec((tm, d), lambda i: (i, 0)),
        out_shape=jax.ShapeDtypeStruct((m, d), out_dtype),
        compiler_params=_params(("parallel",), 32),
        name="rmsnorm",
    )(x, gain.reshape(1, d))


def _rmsnorm_concat_kernel(xp_ref, xs_ref, g_ref, o_ref, x_ref, *, eps, prompt_tiles):
    def emit(src_ref):
        x = src_ref[...]
        ms = jnp.mean(x * x, axis=-1, keepdims=True)
        o_ref[...] = (x * lax.rsqrt(ms + eps) * g_ref[...]).astype(o_ref.dtype)
        x_ref[...] = x

    i = pl.program_id(0)
    pl.when(i < prompt_tiles)(lambda: emit(xp_ref))
    pl.when(i >= prompt_tiles)(lambda: emit(xs_ref))


def _rmsnorm_concat(xp, xs, gain, tm):
    d = xp.shape[1]
    pt = xp.shape[0] // tm
    m = xp.shape[0] + xs.shape[0]
    row_spec = pl.BlockSpec((tm, d), lambda i: (i, 0))
    return pl.pallas_call(
        functools.partial(_rmsnorm_concat_kernel, eps=EPS, prompt_tiles=pt),
        grid=(m // tm,),
        in_specs=[pl.BlockSpec((tm, d), lambda i: (jnp.minimum(i, pt - 1), 0)),
                  pl.BlockSpec((tm, d), lambda i: (jnp.maximum(i - pt, 0), 0)),
                  pl.BlockSpec((1, d), lambda i: (0, 0))],
        out_specs=[row_spec, row_spec],
        out_shape=[jax.ShapeDtypeStruct((m, d), BF16), jax.ShapeDtypeStruct((m, d), F32)],
        compiler_params=_params(("arbitrary",), 40),
        name="rmsnorm_concat",
    )(xp, xs, gain.reshape(1, d))


def _rmsnorm_split_kernel(x_ref, g_ref, op_ref, os_ref, *, eps, prompt_tiles):
    x = x_ref[...]
    ms = jnp.mean(x * x, axis=-1, keepdims=True)
    y = x * lax.rsqrt(ms + eps) * g_ref[...]
    i = pl.program_id(0)

    @pl.when(i < prompt_tiles)
    def _():
        op_ref[...] = y

    @pl.when(i >= prompt_tiles)
    def _():
        os_ref[...] = y


def _rmsnorm_split(x, gain, n_prompt, tm):
    m, d = x.shape
    pt = n_prompt // tm
    return pl.pallas_call(
        functools.partial(_rmsnorm_split_kernel, eps=EPS, prompt_tiles=pt),
        grid=(m // tm,),
        in_specs=[pl.BlockSpec((tm, d), lambda i: (i, 0)),
                  pl.BlockSpec((1, d), lambda i: (0, 0))],
        out_specs=[pl.BlockSpec((tm, d), lambda i: (jnp.minimum(i, pt - 1), 0)),
                   pl.BlockSpec((tm, d), lambda i: (jnp.maximum(i - pt, 0), 0))],
        out_shape=[jax.ShapeDtypeStruct((n_prompt, d), F32), jax.ShapeDtypeStruct((m - n_prompt, d), F32)],
        compiler_params=_params(("arbitrary",), 40),
        name="rmsnorm_split",
    )(x, gain.reshape(1, d))


def _store_row_rstd(x_ref, rstd_ref):
    x = x_ref[...].astype(F32)
    ms = jnp.mean(x * x, axis=-1, keepdims=True)
    rstd_ref[...] = jnp.broadcast_to(lax.rsqrt(ms + EPS), rstd_ref.shape)


def _gate_up_kernel(x_ref, wg_ref, wu_ref, o_ref, *maybe_rstd_ref):
    for rstd_ref in maybe_rstd_ref:
        pl.when(pl.program_id(1) == 0)(functools.partial(_store_row_rstd, x_ref, rstd_ref))
    x = x_ref[...]
    g = jnp.dot(x, wg_ref[...], preferred_element_type=F32)
    u = jnp.dot(x, wu_ref[...], preferred_element_type=F32)
    for c in range(o_ref.shape[1] // HEAD_DIM):
        cols = slice(c * HEAD_DIM, (c + 1) * HEAD_DIM)
        gc, uc = g[:, cols], u[:, cols]
        for rstd_ref in maybe_rstd_ref:
            gc, uc = gc * rstd_ref[...], uc * rstd_ref[...]
        o_ref[:, cols] = (0.5 * jax.nn.silu(gc) * uc).astype(o_ref.dtype)


def _gate_up(x, wg, wu, tm, tn, normalize, col0, ncols):
    m, d = x.shape
    j0 = col0 // tn
    return pl.pallas_call(
        _gate_up_kernel,
        scratch_shapes=[pltpu.VMEM((tm, HEAD_DIM), F32)] if normalize else [],
        grid=(m // tm, ncols // tn),
        in_specs=[pl.BlockSpec((tm, d), lambda i, j: (i, 0)),
                  pl.BlockSpec((d, tn), lambda i, j: (0, j0 + j)),
                  pl.BlockSpec((d, tn), lambda i, j: (0, j0 + j))],
        out_specs=pl.BlockSpec((tm, tn), lambda i, j: (i, j)),
        out_shape=jax.ShapeDtypeStruct((m, ncols), BF16),
        compiler_params=_params(("parallel", "arbitrary"), 52),
        name="ffn_gate_up",
    )(x, wg, wu)


def _down_res_kernel(a_ref, at_ref, w_ref, r_ref, o_ref, *maybe_ob_ref):
    main = a_ref.shape[1]
    y = jnp.dot(a_ref[...], w_ref[:main, :], preferred_element_type=F32)
    y += jnp.dot(at_ref[...], w_ref[main:, :], preferred_element_type=F32)
    y = r_ref[...] + y
    o_ref[...] = y
    for ob_ref in maybe_ob_ref:
        ob_ref[...] = y.astype(ob_ref.dtype)


def _down_res(a, a_tail, w, res, tm, tn, emit_bf16):
    m, f_main = a.shape
    f_tail = a_tail.shape[1]
    f = f_main + f_tail
    d = w.shape[1]
    out_spec = pl.BlockSpec((tm, tn), lambda i, j: (i, j))
    out_specs, out_shape = [out_spec], [jax.ShapeDtypeStruct((m, d), F32)]
    if emit_bf16:
        out_specs.append(out_spec)
        out_shape.append(jax.ShapeDtypeStruct((m, d), BF16))
    return pl.pallas_call(
        _down_res_kernel,
        grid=(m // tm, d // tn),
        in_specs=[pl.BlockSpec((tm, f_main), lambda i, j: (i, 0)),
                  pl.BlockSpec((tm, f_tail), lambda i, j: (i, 0)),
                  pl.BlockSpec((f, tn), lambda i, j: (0, j)),
                  pl.BlockSpec((tm, tn), lambda i, j: (i, j))],
        out_specs=out_specs,
        out_shape=out_shape,
        compiler_params=_params(("parallel", "arbitrary"), 56),
        name="ffn_down_res",
    )(a, a_tail, w, res)


def _swiglu_res(x, res, wg, wu, wd, tm_up, down_tile, normalize, emit_bf16):
    f = wg.shape[1]
    f_main = (f // GATE_UP_TN) * GATE_UP_TN
    a = _gate_up(x, wg, wu, tm_up, GATE_UP_TN, normalize, 0, f_main)
    a_tail = _gate_up(x, wg, wu, tm_up, f - f_main, normalize, f_main, f - f_main)
    return _down_res(a, a_tail, wd, res, *down_tile, emit_bf16)


def _in_proj_kernel(x_ref, w_ref, cs_ref, cos_ref, sa_ref, sb_ref, o_ref, rstd_ref, *, rope_lo, rope_hi):
    j = pl.program_id(1)
    tn = w_ref.shape[1]
    is_rope = jnp.logical_and(j >= rope_lo, j < rope_hi)
    pl.when(j == 0)(functools.partial(_store_row_rstd, x_ref, rstd_ref))

    def body(rope):
        x = x_ref[...]
        for sub in range(tn // PROJ_SUB):
            cols = slice(sub * PROJ_SUB, (sub + 1) * PROJ_SUB)
            y = jnp.dot(x, w_ref[:, cols], preferred_element_type=F32) * cs_ref[:, cols]
            for c in range(PROJ_SUB // HEAD_DIM):
                yc = y[:, c * HEAD_DIM:(c + 1) * HEAD_DIM] * rstd_ref[...]
                if rope:
                    upper = pltpu.roll(yc, HEAD_DIM - ROPE_DIM // 2, 1)
                    lower = pltpu.roll(yc, ROPE_DIM // 2, 1)
                    yc = yc * cos_ref[...] + upper * sa_ref[...] + lower * sb_ref[...]
                o_ref[sub * (PROJ_SUB // HEAD_DIM) + c] = yc.astype(o_ref.dtype)

    pl.when(is_rope)(lambda: body(True))
    pl.when(jnp.logical_not(is_rope))(lambda: body(False))


def _in_proj(x, w, col_scale, rope_tabs, tm, tn):
    m, d = x.shape
    n = w.shape[1]
    cpb = tn // HEAD_DIM
    rope_lo = (3 * NA_WIDTH) // tn
    rope_hi = (3 * NA_WIDTH + 2 * DIFF_WIDTH) // tn
    tab_spec = pl.BlockSpec((tm, HEAD_DIM), lambda i, j: (i, 0))
    return pl.pallas_call(
        functools.partial(_in_proj_kernel, rope_lo=rope_lo, rope_hi=rope_hi),
        grid=(m // tm, n // tn),
        in_specs=[pl.BlockSpec((tm, d), lambda i, j: (i, 0)),
                  pl.BlockSpec((d, tn), lambda i, j: (0, j)),
                  pl.BlockSpec((1, tn), lambda i, j: (0, j)),
                  tab_spec, tab_spec, tab_spec],
        out_specs=pl.BlockSpec((cpb, tm, HEAD_DIM), lambda i, j: (j, i, 0)),
        out_shape=jax.ShapeDtypeStruct((n // HEAD_DIM, m, HEAD_DIM), BF16),
        scratch_shapes=[pltpu.VMEM((tm, HEAD_DIM), F32)],
        compiler_params=_params(("parallel", "arbitrary"), 56),
        name="in_proj_rope",
    )(x, w, col_scale, *rope_tabs)


def _rope_tables(pos):
    half = ROPE_DIM // 2
    inv = ROPE_THETA ** (-2.0 * jnp.arange(half, dtype=F32) / ROPE_DIM)
    ang = pos.astype(F32)[:, None] * inv[None, :]
    cos, sin = jnp.cos(ang), jnp.sin(ang)
    n = pos.shape[0]
    ones = jnp.ones((n, HEAD_DIM - ROPE_DIM), F32)
    zeros_h = jnp.zeros((n, half), F32)
    zeros_r = jnp.zeros((n, HEAD_DIM - ROPE_DIM), F32)
    cos_t = jnp.concatenate([cos, cos, ones], axis=1)
    sa_t = jnp.concatenate([-sin, zeros_h, zeros_r], axis=1)
    sb_t = jnp.concatenate([zeros_h, sin, zeros_r], axis=1)
    return cos_t, sa_t, sb_t


def _na_bias_kernel(rpb_ref, o_ref):
    h = pl.program_id(0)
    shape = (GRID_W, 2 * GRID_W)
    c = lax.broadcasted_iota(jnp.int32, shape, 0)
    lane = lax.broadcasted_iota(jnp.int32, shape, 1)
    second = lane >= GRID_W
    cc = jnp.bitwise_and(lane, GRID_W - 1)
    rel = cc - c + (WIN_W - 1)
    c0 = jnp.clip(c - WIN_W // 2, 0, GRID_W - WIN_W)
    col_ok = jnp.logical_and(cc >= c0, cc < c0 + WIN_W)
    n_dr = 2 * WIN_H - 1
    n_dc = 2 * WIN_W - 1
    base = h * (n_dr * n_dc)

    pair = []
    for d in range(n_dr - 1):
        val = jnp.zeros(shape, F32)
        for k in range(n_dc):
            w0 = rpb_ref[base + d * n_dc + k]
            w1 = rpb_ref[base + (d + 1) * n_dc + k]
            val = jnp.where(rel == k, jnp.where(second, w1, w0), val)
        pair.append(jnp.where(col_ok, val, NEG))

    key_rows = 3 * NA_ROWS_Q
    row_of_lane = second.astype(jnp.int32)
    for qr in range(NA_ROWS_Q):
        for jp in range(key_rows // 2):
            j = 2 * jp + row_of_lane
            tile = pair[2 * jp + (NA_ROWS_Q - 1) - qr]
            rows = pl.ds(qr * GRID_W, GRID_W)
            cols = pl.ds(jp * 2 * GRID_W, 2 * GRID_W)
            first_ok = j >= NA_ROWS_Q
            mid_ok = jnp.logical_and(j >= qr, j <= qr + WIN_H - 1)
            last_ok = j <= WIN_H - 1
            o_ref[0, 0, rows, cols] = jnp.where(first_ok, tile, NEG)
            o_ref[1, 0, rows, cols] = jnp.where(mid_ok, tile, NEG)
            o_ref[2, 0, rows, cols] = jnp.where(last_ok, tile, NEG)


def _na_bias(rpb):
    return pl.pallas_call(
        _na_bias_kernel,
        grid=(NA_HEADS,),
        in_specs=[pl.BlockSpec(memory_space=pltpu.SMEM)],
        out_specs=pl.BlockSpec((3, 1, NA_Q, NA_KEYS), lambda h: (0, h, 0, 0)),
        out_shape=jax.ShapeDtypeStruct((3, NA_HEADS, NA_Q, NA_KEYS), F32),
        compiler_params=_params(("arbitrary",), 32),
        name="na_bias_table",
    )(rpb.reshape(-1))


def _na_kernel(q_ref, k0_ref, k1_ref, k2_ref, v0_ref, v1_ref, v2_ref, km_ref, vm_ref, b_ref, g_ref, o_ref):
    lane = lax.broadcasted_iota(jnp.int32, (1, META_PAD), 1)
    meta_mask = jnp.where(lane < N_META, 0.0, NEG).astype(F32)

    for hh in range(q_ref.shape[0]):
        q = q_ref[hh]
        s = [lax.dot_general(q, k_ref[hh], NT_DIMS, preferred_element_type=F32)
             + b_ref[0, hh, :, j * NA_Q:(j + 1) * NA_Q]
             for j, k_ref in enumerate((k0_ref, k1_ref, k2_ref))]
        s.append(lax.dot_general(q, km_ref[hh], NT_DIMS, preferred_element_type=F32) + meta_mask)

        mx = functools.reduce(jnp.maximum, [jnp.max(x, axis=1, keepdims=True) for x in s])
        p = [jnp.exp(x - mx) for x in s]
        vals = [jnp.concatenate([v, jnp.ones(v.shape, v.dtype)], axis=1)
                for v in (v0_ref[hh], v1_ref[hh], v2_ref[hh], vm_ref[hh])]
        oe = functools.reduce(jnp.add, [jnp.dot(x.astype(BF16), v, preferred_element_type=F32)
                                        for x, v in zip(p, vals)])
        o = oe[:, :HEAD_DIM] * (1.0 / oe[:, HEAD_DIM:])
        ms = jnp.mean(o * o, axis=-1, keepdims=True)
        o_ref[:, hh * HEAD_DIM:(hh + 1) * HEAD_DIM] = (o * lax.rsqrt(ms + EPS) * g_ref[hh]).astype(o_ref.dtype)


def _na_edge_type(g):
    local = jnp.where(g < NA_PROMPT_GROUPS, g, jnp.bitwise_and(g - NA_PROMPT_GROUPS, NA_SAMPLE_GROUPS - 1))
    last = jnp.where(g < NA_PROMPT_GROUPS, NA_PROMPT_GROUPS - 1, NA_SAMPLE_GROUPS - 1)
    return jnp.where(local == 0, 0, jnp.where(local == last, 2, 1))


def _na_attention(p3, pm3, bias, gain):
    hps = NA_HEADS_PER_STEP
    steps_per_group = NA_HEADS // hps

    def tok_spec(block0, shift):
        def index(hb, g):
            return (block0 + hb, jnp.clip(g + shift, 0, NA_GROUPS - 1), 0)
        return pl.BlockSpec((hps, NA_Q, HEAD_DIM), index)

    def meta_spec(block0):
        return pl.BlockSpec((hps, META_PAD, HEAD_DIM), lambda hb, g: (block0 + hb, 0, 0))

    k0, v0 = steps_per_group, 2 * steps_per_group
    return pl.pallas_call(
        _na_kernel,
        grid=(steps_per_group, NA_GROUPS),
        in_specs=[tok_spec(0, 0),
                  tok_spec(k0, -1), tok_spec(k0, 0), tok_spec(k0, 1),
                  tok_spec(v0, -1), tok_spec(v0, 0), tok_spec(v0, 1),
                  meta_spec(k0), meta_spec(v0),
                  pl.BlockSpec((1, hps, NA_Q, NA_KEYS), lambda hb, g: (_na_edge_type(g), hb, 0, 0)),
                  pl.BlockSpec((hps, 1, HEAD_DIM), lambda hb, g: (hb, 0, 0))],
        out_specs=pl.BlockSpec((NA_Q, hps * HEAD_DIM), lambda hb, g: (g, hb)),
        out_shape=jax.ShapeDtypeStruct((T_REAL, NA_WIDTH), BF16),
        compiler_params=_params(("parallel", "arbitrary"), 40),
        name="na_attention",
    )(p3, p3, p3, p3, p3, p3, p3, pm3, pm3, bias, gain.reshape(NA_HEADS, 1, HEAD_DIM))


def _diff_kernel(lq1_ref, lk1_ref, lq2_ref, lk2_ref, g_ref, q_ref, k_ref, v_ref,
                 km_ref, vm_ref, o_ref, s_s, sm_s, m_s, l_s, acc_s):
    tq = q_ref.shape[1]
    n_chunks = k_ref.shape[1] // DIFF_TK
    lane_tiles = DIFF_TK // HEAD_DIM
    lane = lax.broadcasted_iota(jnp.int32, (1, META_PAD), 1)
    meta_mask = jnp.where(lane < N_META, 0.0, NEG).astype(F32)

    for mp in range(2):
        for r0 in range(0, tq, DIFF_STREAM_ROWS):
            qrows = slice(r0, r0 + DIFF_STREAM_ROWS)
            q = q_ref[mp, qrows, :]
            s = lax.dot_general(q, km_ref[mp], NT_DIMS, preferred_element_type=F32) + meta_mask
            sm_s[mp, qrows, :] = s
            mrun = s
            for c in range(n_chunks):
                krows = slice(c * DIFF_TK, (c + 1) * DIFF_TK)
                s = lax.dot_general(q, k_ref[mp, krows, :], NT_DIMS, preferred_element_type=F32)
                s_s[mp, c, qrows, :] = s
                tiles = [s[:, t * HEAD_DIM:(t + 1) * HEAD_DIM] for t in range(lane_tiles)]
                mrun = jnp.maximum(mrun, functools.reduce(jnp.maximum, tiles))
            m_s[mp, qrows, :] = jnp.broadcast_to(jnp.max(mrun, axis=1, keepdims=True),
                                                 (DIFF_STREAM_ROWS, HEAD_DIM))

    pm = [jnp.exp2(sm_s[mp] - m_s[mp]) for mp in range(2)]
    vm = jnp.concatenate([vm_ref[0], vm_ref[1]], axis=1)
    for mp in range(2):
        l_s[mp] = pm[mp]
    acc_s[...] = jnp.dot(jnp.concatenate(pm, axis=0).astype(BF16), vm, preferred_element_type=F32)

    for c in range(n_chunks):
        krows = slice(c * DIFF_TK, (c + 1) * DIFF_TK)
        v = jnp.concatenate([v_ref[0, krows, :], v_ref[1, krows, :]], axis=1)
        ps = []
        for mp in range(2):
            m_b = m_s[mp]
            lsum = l_s[mp]
            parts = []
            for t in range(lane_tiles):
                p = jnp.exp2(s_s[mp, c, :, t * HEAD_DIM:(t + 1) * HEAD_DIM] - m_b)
                lsum = lsum + p
                parts.append(p.astype(BF16))
            l_s[mp] = lsum
            ps.append(jnp.concatenate(parts, axis=1))
        acc_s[...] += jnp.dot(jnp.concatenate(ps, axis=0), v, preferred_element_type=F32)

    lam = (jnp.exp(jnp.sum(lq1_ref[...] * lk1_ref[...], axis=1, keepdims=True))
           - jnp.exp(jnp.sum(lq2_ref[...] * lk2_ref[...], axis=1, keepdims=True)) + LAM_INIT)
    l1 = jnp.sum(l_s[0], axis=1, keepdims=True)
    l2 = jnp.sum(l_s[1], axis=1, keepdims=True)
    o = acc_s[:tq, :] * (1.0 / l1) - acc_s[tq:, :] * (lam / l2)
    ms = jnp.mean(o * o, axis=-1, keepdims=True)
    o_ref[...] = ((o * lax.rsqrt(ms + SUBLN_EPS) * g_ref[...]) * (1.0 - LAM_INIT)).astype(o_ref.dtype)


def _diff_attention(p3, pm3, lq1, lk1, lq2, lk2, gain, row0, seq_len, n_seq, tq):
    q0 = (3 * NA_WIDTH) // (2 * HEAD_DIM)
    k0 = q0 + DIFF_HEADS
    v0 = k0 + DIFF_HEADS
    qt_per_seq = seq_len // tq
    n_chunks = seq_len // DIFF_TK

    def kv_spec(chunk0):
        return pl.BlockSpec((2, seq_len, HEAD_DIM),
                            lambda h, qt: (chunk0 + h, row0 // seq_len + qt // qt_per_seq, 0))

    def meta_spec(chunk0):
        return pl.BlockSpec((2, META_PAD, HEAD_DIM), lambda h, qt: (chunk0 + h, 0, 0))

    vec_spec = pl.BlockSpec((1, HEAD_DIM), lambda h, qt: (0, 0))
    return pl.pallas_call(
        _diff_kernel,
        grid=(DIFF_HEADS, n_seq * qt_per_seq),
        in_specs=[vec_spec, vec_spec, vec_spec, vec_spec,
                  pl.BlockSpec((1, 2 * HEAD_DIM), lambda h, qt: (0, 0)),
                  pl.BlockSpec((2, tq, HEAD_DIM), lambda h, qt: (q0 + h, row0 // tq + qt, 0)),
                  kv_spec(k0), kv_spec(v0), meta_spec(k0), meta_spec(v0)],
        out_specs=pl.BlockSpec((tq, 2 * HEAD_DIM), lambda h, qt: (qt, h)),
        out_shape=jax.ShapeDtypeStruct((n_seq * seq_len, DIFF_WIDTH), BF16),
        scratch_shapes=[pltpu.VMEM((2, n_chunks, tq, DIFF_TK), F32),
                        pltpu.VMEM((2, tq, META_PAD), F32),
                        pltpu.VMEM((2, tq, HEAD_DIM), F32),
                        pltpu.VMEM((2, tq, HEAD_DIM), F32),
                        pltpu.VMEM((2 * tq, 2 * HEAD_DIM), F32)],
        compiler_params=_params(("parallel", "arbitrary"), 52),
        name="diff_attention",
    )(lq1.reshape(1, -1), lk1.reshape(1, -1), lq2.reshape(1, -1), lk2.reshape(1, -1),
      gain.reshape(1, -1), p3, p3, p3, pm3, pm3)


def _out_proj_kernel(na_ref, dp_ref, ds_ref, w_ref, r_ref, o_ref, ob_ref, *, prompt_tiles):
    def body(d_ref):
        y = jnp.dot(na_ref[...], w_ref[:NA_WIDTH, :], preferred_element_type=F32)
        y += jnp.dot(d_ref[...], w_ref[NA_WIDTH:, :], preferred_element_type=F32)
        y = r_ref[...] + y
        o_ref[...] = y
        ob_ref[...] = y.astype(ob_ref.dtype)

    i = pl.program_id(0)
    pl.when(i < prompt_tiles)(lambda: body(dp_ref))
    pl.when(i >= prompt_tiles)(lambda: body(ds_ref))


def _out_proj(na, d_prompt, d_sample, w, res, tm, tn):
    m = na.shape[0]
    n = w.shape[1]
    pt = d_prompt.shape[0] // tm
    out_spec = pl.BlockSpec((tm, tn), lambda i, j: (i, j))
    return pl.pallas_call(
        functools.partial(_out_proj_kernel, prompt_tiles=pt),
        grid=(m // tm, n // tn),
        in_specs=[pl.BlockSpec((tm, NA_WIDTH), lambda i, j: (i, 0)),
                  pl.BlockSpec((tm, DIFF_WIDTH), lambda i, j: (jnp.minimum(i, pt - 1), 0)),
                  pl.BlockSpec((tm, DIFF_WIDTH), lambda i, j: (jnp.maximum(i - pt, 0), 0)),
                  pl.BlockSpec((NA_WIDTH + DIFF_WIDTH, tn), lambda i, j: (0, j)),
                  pl.BlockSpec((tm, tn), lambda i, j: (i, j))],
        out_specs=[out_spec, out_spec],
        out_shape=[jax.ShapeDtypeStruct((m, n), F32), jax.ShapeDtypeStruct((m, n), BF16)],
        compiler_params=_params(("parallel", "arbitrary"), 54),
        name="out_proj_res",
    )(na, d_prompt, d_sample, w, res)


def kernel(x_prompt, x_sample, meta_tokens, ffn1_norm, ffn1_wg, ffn1_wu, ffn1_wd, mix_norm, w_in, na_rpb, na_norm, diff_lq1, diff_lk1, diff_lq2, diff_lk2, diff_norm, w_out, ffn2_norm, ffn2_wg, ffn2_wu, ffn2_wd, final_norm):
    xp = x_prompt.reshape(PROMPT_LEN, D_MODEL)
    xs = x_sample.reshape(N_SAMPLES * SAMPLE_LEN, D_MODEL)
    meta = jnp.pad(meta_tokens, ((0, META_PAD - N_META), (0, 0)))

    wg1, wu1, wd1 = ffn1_wg[0].astype(BF16), ffn1_wu[0].astype(BF16), ffn1_wd[0].astype(BF16)
    gain2 = ffn2_norm[0].astype(F32)[:, None]
    wg2, wu2, wd2 = (gain2 * ffn2_wg[0]).astype(BF16), (gain2 * ffn2_wu[0]).astype(BF16), ffn2_wd[0].astype(BF16)
    w_in_b = (mix_norm[0].astype(F32)[:, None] * w_in[0]).astype(BF16)
    w_out_b = w_out[0].astype(BF16)

    col = jnp.arange(IN_WIDTH)
    is_q_na = col < NA_WIDTH
    is_q_diff = jnp.logical_and(col >= 3 * NA_WIDTH, col < 3 * NA_WIDTH + DIFF_WIDTH)
    col_scale = jnp.where(is_q_na, ATTN_SCALE, jnp.where(is_q_diff, ATTN_SCALE * LOG2E, 1.0))
    col_scale = col_scale.astype(F32).reshape(1, IN_WIDTH)

    pos_real = jnp.concatenate([jnp.arange(PROMPT_LEN, dtype=jnp.int32)]
                               + [jnp.arange(SAMPLE_LEN, dtype=jnp.int32)] * N_SAMPLES) + N_META
    pos_meta = jnp.arange(META_PAD, dtype=jnp.int32)

    xn1, x = _rmsnorm_concat(xp, xs, ffn1_norm[0], 256)
    h1, h1_b = _swiglu_res(xn1, x, wg1, wu1, wd1, 1024, (512, 512), normalize=False, emit_bf16=True)
    xnm1 = _rmsnorm(meta, ffn1_norm[0], BF16, META_PAD)
    _, h1m_b = _swiglu_res(xnm1, meta, wg1, wu1, wd1, META_PAD, (META_PAD, 256), normalize=False, emit_bf16=True)

    p3 = _in_proj(h1_b, w_in_b, col_scale, _rope_tables(pos_real), 1024, 1024)
    pm3 = _in_proj(h1m_b, w_in_b, col_scale, _rope_tables(pos_meta), META_PAD, 1024)
    bias = _na_bias(na_rpb[0])
    na = _na_attention(p3, pm3, bias, na_norm[0])
    lam_args = (diff_lq1[0], diff_lk1[0], diff_lq2[0], diff_lk2[0], diff_norm[0])
    dd_prompt = _diff_attention(p3, pm3, *lam_args, row0=0, seq_len=PROMPT_LEN, n_seq=1, tq=256)
    dd_sample = _diff_attention(p3, pm3, *lam_args, row0=PROMPT_LEN, seq_len=SAMPLE_LEN, n_seq=N_SAMPLES, tq=512)
    h2, h2_b = _out_proj(na, dd_prompt, dd_sample, w_out_b, h1, 1024, 512)

    h3, = _swiglu_res(h2_b, h2, wg2, wu2, wd2, 1024, (512, 512), normalize=True, emit_bf16=False)
    y_prompt, y_sample = _rmsnorm_split(h3, final_norm, PROMPT_LEN, 256)
    return (y_prompt.reshape(1, PROMPT_LEN, D_MODEL), y_sample.reshape(N_SAMPLES, SAMPLE_LEN, D_MODEL))
```
